```python
import math
import jax, jax.numpy as jnp
from jax import lax
import numpy as np

D_MODEL = 1024
BATCH = 2
SEQ = 16384
DEPTH = 2
DEC_BATCH = 8
DEC_SEQ = 8192
PAST_LEN = 128

D_MIX = 2 * D_MODEL
A_WIDTH = D_MIX // 4
A_HEAD_DIM = 64
A_HEADS = A_WIDTH // A_HEAD_DIM
DILATED_CFGS = ((128, 1), (512, 4), (2048, 16))
ATT_BLOCK = 128
ROPE_THETA = 10000.0
B_WIDTH = D_MIX // 4
B_GROUPS = 4
B_CHUNK = 128
C_WIDTH = D_MIX // 2
C_HEAD_DIM = 64
C_HEADS = C_WIDTH // C_HEAD_DIM
C_STATE = 128
C_GROUPS = 2
C_CONV = 3
C_CHUNK = 128
C_CONV_CH = C_WIDTH + 2 * C_GROUPS * C_STATE
IN_SIZES = (A_WIDTH, A_WIDTH, A_WIDTH, A_WIDTH,
            B_WIDTH, B_WIDTH, B_WIDTH,
            C_CONV_CH, C_WIDTH, 2 * C_HEADS)
IN_WIDTH = sum(IN_SIZES)
EPS = 1e-5
NEG = -1e30

kernel_name = "hybrid_bidir_dilated_sgu_ssd_encoder"


def rmsnorm(x, w):
    x32 = x.astype(jnp.float32)
    y = x32 * lax.rsqrt(jnp.mean(x32 * x32, axis=-1, keepdims=True) + EPS)
    return (y * w.astype(jnp.float32)).astype(x.dtype)


def rope_tables(S, E):
    inv = ROPE_THETA ** (-jnp.arange(0, E, 2, dtype=jnp.float32) / E)
    ang = jnp.arange(S, dtype=jnp.float32)[:, None] * inv[None, :]
    return jnp.cos(ang), jnp.sin(ang)


def rope(t, cos, sin):
    t32 = t.astype(jnp.float32)
    t1, t2 = jnp.split(t32, 2, axis=-1)
    c = cos[None, :, None, :]
    s = sin[None, :, None, :]
    return jnp.concatenate([t1 * c - t2 * s, t2 * c + t1 * s], axis=-1).astype(t.dtype)


def dilated_window_attention(q, k, v, dil, half):
    Bn, S, H, E = q.shape
    Q = ATT_BLOCK
    K = Q + 2 * half
    L = -(-S // dil)
    nb = -(-L // Q)
    Lp = nb * Q
    Sp = Lp * dil

    def to_sub(t):
        t = jnp.pad(t, ((0, 0), (0, Sp - S), (0, 0), (0, 0)))
        return t.reshape(Bn, Lp, dil, H, E).transpose(0, 2, 1, 3, 4)

    qb = to_sub(q).reshape(Bn, dil, nb, Q, H, E)
    side = ((0, 0), (0, 0), (half, half), (0, 0), (0, 0))
    ks = jnp.pad(to_sub(k), side)
    vs = jnp.pad(to_sub(v), side)
    idx = (jnp.arange(nb) * Q)[:, None] + jnp.arange(K)[None, :]
    kb = ks[:, :, idx]
    vb = vs[:, :, idx]
    sub_pos = idx - half
    true_pos = sub_pos[None] * dil + jnp.arange(dil)[:, None, None]
    key_ok = (sub_pos[None] >= 0) & (true_pos < S)
    rel = jnp.arange(K)[None, :] - jnp.arange(Q)[:, None]
    band = (rel >= 0) & (rel <= 2 * half)
    mask = band[None, None] & key_ok[:, :, None, :]

    s = jnp.einsum('bdnqhe,bdnkhe->bdnhqk', qb, kb,
                   preferred_element_type=jnp.float32) * (E ** -0.5)
    s = jnp.where(mask[None, :, :, None], s, NEG)
    m = jnp.max(s, axis=-1, keepdims=True)
    p = jnp.exp(s - m)
    den = jnp.sum(p, axis=-1)
    o = jnp.einsum('bdnhqk,bdnkhe->bdnqhe', p, vb.astype(jnp.float32))
    den_t = jnp.swapaxes(den, -1, -2)
    o = o / den_t[..., None]
    lse = jnp.swapaxes(m[..., 0], -1, -2) + jnp.log(den_t)
    o = o.reshape(Bn, dil, Lp, H, E).transpose(0, 2, 1, 3, 4).reshape(Bn, Sp, H, E)[:, :S]
    lse = lse.reshape(Bn, dil, Lp, H).transpose(0, 2, 1, 3).reshape(Bn, Sp, H)[:, :S]
    return o, lse


def mixer_a(qa, ka, va, cos, sin):
    Bn, S, _ = qa.shape
    shp = (Bn, S, A_HEADS, A_HEAD_DIM)
    q = rope(qa.reshape(shp), cos, sin)
    k = rope(ka.reshape(shp), cos, sin)
    v = va.reshape(shp)
    outs, lses = [], []
    for window, dil in DILATED_CFGS:
        o, l = dilated_window_attention(q, k, v, dil, window // (2 * dil))
        outs.append(o)
        lses.append(l)
    wts = jax.nn.softmax(jnp.stack(lses), axis=0)
    o = jnp.einsum('cbsh,cbshe->bshe', wts, jnp.stack(outs))
    return o.reshape(Bn, S, A_WIDTH).astype(qa.dtype)


def mixer_b(u, v, w_s, b_s):
    Bn, S, _ = u.shape
    nc = S // B_CHUNK
    vg = v.reshape(Bn, nc, B_CHUNK, B_GROUPS, B_WIDTH // B_GROUPS)
    mixed = jnp.einsum('gts,bnsgc->bntgc', w_s, vg) + b_s.T[None, None, :, :, None]
    return u * mixed.reshape(Bn, S, B_WIDTH)


def ssd_scan(x, dt, A, Bm, Cm):
    Bn, S, H, P = x.shape
    G, N = Bm.shape[2], Bm.shape[3]
    R = H // G
    L = C_CHUNK
    nc = S // L
    xc = x.reshape(Bn, nc, L, G, R, P)
    dtc = dt.reshape(Bn, nc, L, G, R)
    Bc = Bm.reshape(Bn, nc, L, G, N)
    Cc = Cm.reshape(Bn, nc, L, G, N)
    acum = jnp.cumsum(dtc * A.reshape(G, R), axis=2)
    seg = acum[:, :, :, None] - acum[:, :, None, :]
    tri = jnp.tril(jnp.ones((L, L), dtype=bool))[None, None, :, :, None, None]
    decay = jnp.exp(jnp.where(tri, seg, -jnp.inf))
    cb = jnp.einsum('bclgn,bcsgn->bclsg', Cc, Bc)
    wmat = cb[..., None] * decay * dtc[:, :, None]
    y_diag = jnp.einsum('bclsgr,bcsgrp->bclgrp', wmat, xc)
    decay_end = jnp.exp(acum[:, :, -1:] - acum)
    xw = xc * (decay_end * dtc)[..., None]
    states = jnp.einsum('bcsgn,bcsgrp->bcgrpn', Bc, xw)
    chunk_decay = jnp.exp(acum[:, :, -1])

    def step(h, inp):
        st, dec = inp
        return h * dec[..., None, None] + st, h

    h0 = jnp.zeros((Bn, G, R, P, N), jnp.float32)
    _, prev = lax.scan(step, h0, (jnp.moveaxis(states, 1, 0), jnp.moveaxis(chunk_decay, 1, 0)))
    prev = jnp.moveaxis(prev, 0, 1)
    y_off = jnp.einsum('bclgn,bcgrpn->bclgrp', Cc, prev) * jnp.exp(acum)[..., None]
    return (y_diag + y_off).reshape(Bn, S, H, P)


def mixer_c(xbc, zc, dt_raw, conv_w, conv_b, dt_bias, a_log, d_skip, ssd_norm_w):
    Bn, S, _ = xbc.shape
    xbc = lax.conv_general_dilated(xbc, conv_w[:, None, :].astype(xbc.dtype), window_strides=(1,),
                                   padding=[(C_CONV // 2, C_CONV // 2)],
                                   dimension_numbers=('NWC', 'WIO', 'NWC'),
                                   feature_group_count=C_CONV_CH) + conv_b
    xbc = jax.nn.silu(xbc.astype(jnp.float32))
    xs, Bm, Cm = jnp.split(xbc, [C_WIDTH, C_WIDTH + C_GROUPS * C_STATE], axis=-1)
    xs = xs.reshape(Bn, S, C_HEADS, C_HEAD_DIM)
    Bm = Bm.reshape(Bn, S, C_GROUPS, C_STATE)
    Cm = Cm.reshape(Bn, S, C_GROUPS, C_STATE)
    dt = jax.nn.softplus(dt_raw.astype(jnp.float32).reshape(Bn, S, 2, C_HEADS)
                         + dt_bias.astype(jnp.float32)[None, None])
    A = -jnp.exp(a_log.astype(jnp.float32))
    y_f = ssd_scan(xs, dt[:, :, 0], A[0], Bm, Cm)
    fl = lambda t: jnp.flip(t, axis=1)
    y_b = fl(ssd_scan(fl(xs), fl(dt[:, :, 1]), A[1], fl(Bm), fl(Cm)))
    y = y_f + y_b + d_skip.astype(jnp.float32)[None, None, :, None] * xs
    y = y.reshape(Bn, S, C_WIDTH) * jax.nn.silu(zc.astype(jnp.float32))
    return rmsnorm(y, ssd_norm_w).astype(zc.dtype)


def layer(h, norm_w, w_in, sgu_w, sgu_b, conv_w, conv_b, dt_bias, a_log, d_skip, ssd_norm_w, w_out):
    Bn, S, _ = h.shape
    xn = rmsnorm(h, norm_w)
    proj = xn @ w_in
    cuts = [int(c) for c in np.cumsum(IN_SIZES)[:-1]]
    qa, ka, va, za, ub, vb, zb, xbc, zc, dt_raw = jnp.split(proj, cuts, axis=-1)
    cos, sin = rope_tables(S, A_HEAD_DIM)
    out_a = mixer_a(qa, ka, va, cos, sin) * jax.nn.silu(za)
    out_b = mixer_b(ub, vb, sgu_w, sgu_b) * jax.nn.silu(zb)
    out_c = mixer_c(xbc, zc, dt_raw, conv_w, conv_b, dt_bias, a_log, d_skip, ssd_norm_w)
    mixed = jnp.concatenate([out_a, out_b, out_c], axis=-1)
    return h + mixed @ w_out


def trunk(x, norm_w, w_in, sgu_w, sgu_b, conv_w, conv_b, dt_bias, a_log, d_skip, ssd_norm_w, w_out, final_norm_w):
    h = x
    for l in range(DEPTH):
        h = layer(h, norm_w[l], w_in[l], sgu_w[l], sgu_b[l], conv_w[l], conv_b[l],
                  dt_bias[l], a_log[l], d_skip[l], ssd_norm_w[l], w_out[l])
    return rmsnorm(h, final_norm_w)


def setup_inputs(seed: int = 0) -> dict:
    key = jax.random.key(seed)
    ks = jax.random.split(key, 16)
    f32 = jnp.float32
    x_prompt = jax.random.normal(ks[0], (BATCH, SEQ, D_MODEL), f32)
    x_sample = jax.random.normal(ks[1], (DEC_BATCH, DEC_SEQ, D_MODEL), f32)
    norm_w = 1.0 + 0.02 * jax.random.normal(ks[2], (DEPTH, D_MODEL), f32)
    w_in = jax.random.normal(ks[3], (DEPTH, D_MODEL, IN_WIDTH), f32) * D_MODEL ** -0.5
    sgu_w = jax.random.normal(ks[4], (DEPTH, B_GROUPS, B_CHUNK, B_CHUNK), f32) * B_CHUNK ** -0.5
    sgu_b = 1.0 + 0.1 * jax.random.normal(ks[5], (DEPTH, B_GROUPS, B_CHUNK), f32)
    conv_w = jax.random.normal(ks[6], (DEPTH, C_CONV, C_CONV_CH), f32) * C_CONV ** -0.5
    conv_b = 0.02 * jax.random.normal(ks[7], (DEPTH, C_CONV_CH), f32)
    dt0 = jnp.exp(jax.random.uniform(ks[8], (DEPTH, 2, C_HEADS), f32,
                                     math.log(1e-3), math.log(1e-1)))
    dt_bias = dt0 + jnp.log(-jnp.expm1(-dt0))
    a_log = jnp.log(jax.random.uniform(ks[9], (DEPTH, 2, C_HEADS), f32, 1.0, 16.0))
    d_skip = 1.0 + 0.1 * jax.random.normal(ks[10], (DEPTH, C_HEADS), f32)
    ssd_norm_w = 1.0 + 0.02 * jax.random.normal(ks[11], (DEPTH, C_WIDTH), f32)
    w_out = jax.random.normal(ks[12], (DEPTH, D_MIX, D_MODEL), f32) * D_MIX ** -0.5
    final_norm_w = 1.0 + 0.02 * jax.random.normal(ks[13], (D_MODEL,), f32)
    return {"x_prompt": x_prompt, "x_sample": x_sample, "norm_w": norm_w, "w_in": w_in,
            "sgu_w": sgu_w, "sgu_b": sgu_b, "conv_w": conv_w, "conv_b": conv_b,
            "dt_bias": dt_bias, "a_log": a_log, "d_skip": d_skip, "ssd_norm_w": ssd_norm_w,
            "w_out": w_out, "final_norm_w": final_norm_w}


def reference(x_prompt, x_sample, norm_w, w_in, sgu_w, sgu_b, conv_w, conv_b, dt_bias, a_log,
              d_skip, ssd_norm_w, w_out, final_norm_w):
    y_prompt = trunk(x_prompt, norm_w, w_in, sgu_w, sgu_b, conv_w, conv_b, dt_bias, a_log,
                     d_skip, ssd_norm_w, w_out, final_norm_w)
    y_sample = trunk(x_sample, norm_w, w_in, sgu_w, sgu_b, conv_w, conv_b, dt_bias, a_log,
                     d_skip, ssd_norm_w, w_out, final_norm_w)
    return (y_prompt, y_sample)
```

```python
import functools

import jax
import jax.numpy as jnp
from jax import lax
from jax.experimental import pallas as pl
from jax.experimental.pallas import tpu as pltpu

F32 = jnp.float32
BF16 = jnp.bfloat16

D_MODEL = 1024
HEAD_DIM = 64
A_WIDTH = 512
A_HEADS = 8
ATT_CFGS = ((128, 1), (512, 4), (2048, 16))
ATT_Q = 128
ATT_HALF = 64
ROPE_THETA = 10000.0
B_WIDTH = 512
B_GROUPS = 4
CHUNK = 128
C_WIDTH = 1024
C_HEADS = 16
C_STATE = 128
C_GROUPS = 2
C_CONV_CH = C_WIDTH + 2 * C_GROUPS * C_STATE
MAIN_COLS = 6144
DT_COLS = 2 * C_HEADS
D_MIX = 2048
EPS = 1e-5
NEG = -1e30

VMEM_LIMIT_BYTES = 56 * 1024 * 1024
LANES = 128
BF16_SUBLANES = 16

TM_PROJ = 512
TS_SEQ = 512
ATT_QB = 512


def _silu(x):
    return x / (1.0 + jnp.exp(-x))


def _softplus(x):
    return jnp.maximum(x, 0.0) + jnp.log1p(jnp.exp(-jnp.abs(x)))


def _split3(x):
    hi = x.astype(BF16)
    r1 = x - hi.astype(F32)
    mid = r1.astype(BF16)
    lo = (r1 - mid.astype(F32)).astype(BF16)
    return hi, mid, lo


def _dot(a, b):
    return jnp.dot(a, b, preferred_element_type=F32)


def _dot_nt(a, b):
    return lax.dot_general(a, b, (((1,), (1,)), ((), ())), preferred_element_type=F32)


def _inproj_kernel(x_ref, nw_ref, cos_ref, sin_ref, w_ref, wdt_ref, wdtT_ref,
                   qkv_ref, zab_ref, xbc_ref, zc_ref, dt_ref, dtT_ref):
    x = x_ref[...]
    ms = jnp.mean(x * x, axis=-1, keepdims=True)
    xn = (x * lax.rsqrt(ms + EPS) * nw_ref[...]).astype(BF16)
    cos = cos_ref[...]
    sin = sin_ref[...]
    lane = lax.broadcasted_iota(jnp.int32, cos.shape, 1)
    first_half = (lane % HEAD_DIM) < (HEAD_DIM // 2)

    def rope_group(col0, scale):
        acc = _dot(xn, w_ref[:, col0:col0 + A_WIDTH])
        for j in range(A_WIDTH // LANES):
            t = acc[:, j * LANES:(j + 1) * LANES]
            fwd = pltpu.roll(t, HEAD_DIM // 2, 1)
            bwd = pltpu.roll(t, LANES - HEAD_DIM // 2, 1)
            rot = jnp.where(first_half, bwd, fwd)
            out = t * cos + rot * sin
            if scale != 1.0:
                out = out * scale
            qkv_ref[:, col0 + j * LANES:col0 + (j + 1) * LANES] = out.astype(BF16)

    rope_group(0, HEAD_DIM ** -0.5)
    rope_group(A_WIDTH, 1.0)
    qkv_ref[:, 2 * A_WIDTH:3 * A_WIDTH] = _dot(xn, w_ref[:, 2 * A_WIDTH:3 * A_WIDTH]).astype(BF16)
    base = 3 * A_WIDTH
    for j in range(4):
        zab_ref[:, j * 512:(j + 1) * 512] = _dot(xn, w_ref[:, base + j * 512:base + (j + 1) * 512]).astype(BF16)
    base += 2048
    for j in range(3):
        xbc_ref[:, j * 512:(j + 1) * 512] = _dot(xn, w_ref[:, base + j * 512:base + (j + 1) * 512]).astype(BF16)
    base += C_CONV_CH
    for j in range(2):
        zc_ref[:, j * 512:(j + 1) * 512] = _dot(xn, w_ref[:, base + j * 512:base + (j + 1) * 512]).astype(BF16)
    dt_ref[...] = _dot(xn, wdt_ref[...])
    dtT = _dot_nt(wdtT_ref[...], xn)
    for c in range(dtT_ref.shape[0]):
        dtT_ref[c] = dtT[:, c * CHUNK:(c + 1) * CHUNK]


def _in_proj(x2d, seq_len, nw, cos_t, sin_t, w_main, w_dt, w_dtT):
    T = x2d.shape[0]
    tm = TM_PROJ
    tiles_per_seq = seq_len // tm
    const = lambda i: (0, 0)
    row = lambda i: (i, 0)
    return pl.pallas_call(
        _inproj_kernel,
        grid=(T // tm,),
        in_specs=[
            pl.BlockSpec((tm, D_MODEL), row),
            pl.BlockSpec((1, D_MODEL), const),
            pl.BlockSpec((tm, LANES), lambda i: (i % tiles_per_seq, 0)),
            pl.BlockSpec((tm, LANES), lambda i: (i % tiles_per_seq, 0)),
            pl.BlockSpec((D_MODEL, MAIN_COLS), const),
            pl.BlockSpec((D_MODEL, DT_COLS), const),
            pl.BlockSpec((DT_COLS, D_MODEL), const),
        ],
        out_specs=[
            pl.BlockSpec((tm, 3 * A_WIDTH), row),
            pl.BlockSpec((tm, 2048), row),
            pl.BlockSpec((tm, C_CONV_CH), row),
            pl.BlockSpec((tm, C_WIDTH), row),
            pl.BlockSpec((tm, DT_COLS), row),
            pl.BlockSpec((tm // CHUNK, DT_COLS, CHUNK), lambda i: (i, 0, 0)),
        ],
        out_shape=[
            jax.ShapeDtypeStruct((T, 3 * A_WIDTH), BF16),
            jax.ShapeDtypeStruct((T, 2048), BF16),
            jax.ShapeDtypeStruct((T, C_CONV_CH), BF16),
            jax.ShapeDtypeStruct((T, C_WIDTH), BF16),
            jax.ShapeDtypeStruct((T, DT_COLS), F32),
            jax.ShapeDtypeStruct((T // CHUNK, DT_COLS, CHUNK), F32),
        ],
        compiler_params=pltpu.CompilerParams(
            dimension_semantics=("arbitrary",), vmem_limit_bytes=VMEM_LIMIT_BYTES),
        name="in_proj",
    )(x2d, nw, cos_t, sin_t, w_main, w_dt, w_dtT)


def _attn_kernel(q_ref, kp_ref, kc_ref, kn_ref, vp_ref, vc_ref, vn_ref, o_ref, lse_ref,
                 kbuf, vbuf, *, qb, sub_len):
    n = pl.program_id(2)
    kbuf[0:ATT_Q, :] = kp_ref[...]
    kbuf[ATT_Q:ATT_Q + qb, :] = kc_ref[...]
    kbuf[ATT_Q + qb:, :] = kn_ref[...]
    vbuf[0:ATT_Q, :] = vp_ref[...]
    vbuf[ATT_Q:ATT_Q + qb, :] = vc_ref[...]
    vbuf[ATT_Q + qb:, :] = vn_ref[...]

    nkeys = ATT_Q + 2 * ATT_HALF
    qq = lax.broadcasted_iota(jnp.int32, (ATT_Q, nkeys), 0)
    kk = lax.broadcasted_iota(jnp.int32, (ATT_Q, nkeys), 1)
    lane = lax.broadcasted_iota(jnp.int32, (ATT_Q, LANES), 1)
    low_half = lane < HEAD_DIM

    def body(i, carry):
        r0 = pl.multiple_of(i * ATT_Q, ATT_Q)
        k0 = pl.multiple_of(i * ATT_Q + ATT_Q - ATT_HALF, ATT_HALF)
        q = q_ref[pl.ds(r0, ATT_Q), :]
        kw = kbuf[pl.ds(k0, nkeys), :]
        vw = vbuf[pl.ds(k0, nkeys), :]
        base = n * qb + i * ATT_Q - ATT_HALF
        lo = jnp.maximum(qq, -base)
        hi = jnp.minimum(qq + 2 * ATT_HALF, sub_len - 1 - base)
        mask = (kk >= lo) & (kk <= hi)
        lse_tile = jnp.zeros((ATT_Q, LANES), F32)
        for j in range(A_WIDTH // LANES):
            cols = slice(j * LANES, (j + 1) * LANES)
            qp, kp, vp = q[:, cols], kw[:, cols], vw[:, cols]
            outs = []
            for e in range(2):
                own = low_half if e == 0 else jnp.logical_not(low_half)
                qe = jnp.where(own, qp, jnp.zeros_like(qp))
                s = _dot_nt(qe, kp)
                s = jnp.where(mask, s, NEG)
                m = jnp.max(s, axis=-1, keepdims=True)
                p = jnp.exp(s - m)
                den = jnp.sum(p, axis=-1, keepdims=True)
                outs.append(_dot(p.astype(BF16), vp) / den)
                lse_tile = jnp.where(lane == 2 * j + e, m + jnp.log(den), lse_tile)
            o_ref[pl.ds(r0, ATT_Q), cols] = jnp.where(low_half, outs[0], outs[1]).astype(BF16)
        lse_ref[pl.ds(r0, ATT_Q), :] = lse_tile
        return carry

    lax.fori_loop(0, qb // ATT_Q, body, 0)


def _attention(qkv3, dil):
    Bn, S, _ = qkv3.shape
    L = S // dil
    qb = min(ATT_QB, L)
    nb = L // qb
    r128 = qb // ATT_Q
    last128 = L // ATT_Q - 1
    view = qkv3.reshape(Bn, L, dil * 3 * A_WIDTH)

    def cur(c):
        return pl.BlockSpec((None, qb, A_WIDTH), lambda b, r, n: (b, n, 3 * r + c))

    def prev(c):
        return pl.BlockSpec((None, ATT_Q, A_WIDTH),
                            lambda b, r, n: (b, jnp.maximum(n * r128 - 1, 0), 3 * r + c))

    def nxt(c):
        return pl.BlockSpec((None, ATT_Q, A_WIDTH),
                            lambda b, r, n: (b, jnp.minimum((n + 1) * r128, last128), 3 * r + c))

    o, lse = pl.pallas_call(
        functools.partial(_attn_kernel, qb=qb, sub_len=L),
        grid=(Bn, dil, nb),
        in_specs=[cur(0), prev(1), cur(1), nxt(1), prev(2), cur(2), nxt(2)],
        out_specs=[
            pl.BlockSpec((None, qb, A_WIDTH), lambda b, r, n: (b, n, r)),
            pl.BlockSpec((None, qb, LANES), lambda b, r, n: (b, n, r)),
        ],
        out_shape=[
            jax.ShapeDtypeStruct((Bn, L, dil * A_WIDTH), BF16),
            jax.ShapeDtypeStruct((Bn, L, dil * LANES), F32),
        ],
        scratch_shapes=[
            pltpu.VMEM((qb + 2 * ATT_Q, A_WIDTH), BF16),
            pltpu.VMEM((qb + 2 * ATT_Q, A_WIDTH), BF16),
        ],
        compiler_params=pltpu.CompilerParams(
            dimension_semantics=("arbitrary", "arbitrary", "arbitrary"),
            vmem_limit_bytes=VMEM_LIMIT_BYTES),
        name=f"attn_d{dil}",
    )(view, view, view, view, view, view, view)
    return o.reshape(Bn, S, A_WIDTH), lse.reshape(Bn, S, LANES)


def _ssd_chunk(xs, bm, cm, dt_raw, dt_rawT, bias_row, bias_col, alog_row, alog_col,
               state_ref, reverse):
    L = CHUNK
    dt = _softplus(dt_raw + bias_row)
    dtT = _softplus(dt_rawT + bias_col)
    dtA = dt * (-jnp.exp(alog_row))
    dtAT = dtT * (-jnp.exp(alog_col))
    ri = lax.broadcasted_iota(jnp.int32, (L, L), 0)
    ci = lax.broadcasted_iota(jnp.int32, (L, L), 1)
    tri = (ci >= ri) if reverse else (ci <= ri)
    tri_bf = jnp.where(tri, 1.0, 0.0).astype(BF16)
    acum = sum(_dot(tri_bf, t) for t in _split3(dtA))
    acumT = sum(_dot_nt(t, tri_bf) for t in _split3(dtAT))
    e = 0 if reverse else L - 1
    edge_row = acum[e:e + 1, :]
    edge_col = acumT[:, e:e + 1]
    wT = jnp.exp(edge_col - acumT) * dtT
    eac = jnp.exp(acum)
    cdec = jnp.exp(edge_row)
    lane = lax.broadcasted_iota(jnp.int32, (L, LANES), 1)
    low_half = lane < HEAD_DIM

    ys = []
    for g in range(C_GROUPS):
        bg = bm[:, g * C_STATE:(g + 1) * C_STATE]
        cg = cm[:, g * C_STATE:(g + 1) * C_STATE]
        cb = _dot_nt(cg, bg)
        bgT = bg.astype(F32).T
        cg32 = cg.astype(F32)
        for pr in range(4):
            pair = g * 4 + pr
            cols = slice(pair * LANES, (pair + 1) * LANES)
            xp = xs[:, cols]
            prev = state_ref[:, cols]
            rhs = jnp.concatenate([xp, prev.astype(BF16)], axis=0)
            yh, sh, cd = [], [], []
            for k in range(2):
                h = 2 * pair + k
                seg = acum[:, h:h + 1] - acumT[h:h + 1, :]
                dec = jnp.exp(jnp.where(tri, seg, NEG))
                w = (cb * dec * dtT[h:h + 1, :]).astype(BF16)
                cs = (cg32 * eac[:, h:h + 1]).astype(BF16)
                yh.append(_dot(jnp.concatenate([w, cs], axis=1), rhs))
                bsT = (bgT * wT[h:h + 1, :]).astype(BF16)
                sh.append(_dot(bsT, xp))
                cd.append(cdec[:, h:h + 1])
            ys.append(jnp.where(low_half, yh[0], yh[1]))
            low1 = low_half[0:1, :]
            decay = jnp.where(low1, cd[0], cd[1])
            state_ref[:, cols] = prev * decay + jnp.where(low_half, sh[0], sh[1])
    return ys


def _ssd_fwd_kernel(xbc_ref, hp_ref, hn_ref, dt_ref, dtT_ref, cw_ref, cb_ref,
                    brow_ref, bcol_ref, arow_ref, acol_ref, dskip_ref,
                    xact_ref, yf_ref, state_ref, *, ts):
    s = pl.program_id(1)
    ns = pl.num_programs(1)

    @pl.when(s == 0)
    def _():
        state_ref[...] = jnp.zeros_like(state_ref)

    x = xbc_ref[...].astype(F32)
    prev_row = jnp.where(s > 0, hp_ref[BF16_SUBLANES - 1:BF16_SUBLANES, :].astype(F32), 0.0)
    next_row = jnp.where(s < ns - 1, hn_ref[0:1, :].astype(F32), 0.0)
    row = lax.broadcasted_iota(jnp.int32, x.shape, 0)
    xm1 = jnp.where(row == 0, prev_row, pltpu.roll(x, 1, 0))
    xp1 = jnp.where(row == ts - 1, next_row, pltpu.roll(x, ts - 1, 0))
    conv = cw_ref[0:1, :] * xm1 + cw_ref[1:2, :] * x + cw_ref[2:3, :] * xp1 + cb_ref[...]
    xact_ref[...] = _silu(conv).astype(BF16)

    brow, bcol = brow_ref[:, 0:C_HEADS], bcol_ref[0:C_HEADS, :]
    arow, acol = arow_ref[:, 0:C_HEADS], acol_ref[0:C_HEADS, :]

    def body(c, carry):
        r0 = pl.multiple_of(c * CHUNK, CHUNK)
        rows = pl.ds(r0, CHUNK)
        xs = xact_ref[rows, 0:C_WIDTH]
        bm = xact_ref[rows, C_WIDTH:C_WIDTH + C_GROUPS * C_STATE]
        cm = xact_ref[rows, C_WIDTH + C_GROUPS * C_STATE:C_CONV_CH]
        ys = _ssd_chunk(xs, bm, cm, dt_ref[rows, 0:C_HEADS], dtT_ref[c, 0:C_HEADS, :],
                        brow, bcol, arow, acol, state_ref, reverse=False)
        for pair, y in enumerate(ys):
            cols = slice(pair * LANES, (pair + 1) * LANES)
            y = y + dskip_ref[:, cols] * xs[:, cols].astype(F32)
            yf_ref[rows, cols] = y.astype(BF16)
        return carry

    lax.fori_loop(0, ts // CHUNK, body, 0)


def _ssd_fwd(xbc3, dt3, dtT, conv_w, conv_b, brow, bcol, arow, acol, dskip):
    Bn, S, _ = xbc3.shape
    ts = TS_SEQ
    ns = S // ts
    hb = ts // BF16_SUBLANES
    last_hb = S // BF16_SUBLANES - 1
    const = lambda b, s: (0, 0)
    tile = lambda b, s: (b, s, 0)
    return pl.pallas_call(
        functools.partial(_ssd_fwd_kernel, ts=ts),
        grid=(Bn, ns),
        in_specs=[
            pl.BlockSpec((None, ts, C_CONV_CH), tile),
            pl.BlockSpec((None, BF16_SUBLANES, C_CONV_CH),
                         lambda b, s: (b, jnp.maximum(s * hb - 1, 0), 0)),
            pl.BlockSpec((None, BF16_SUBLANES, C_CONV_CH),
                         lambda b, s: (b, jnp.minimum((s + 1) * hb, last_hb), 0)),
            pl.BlockSpec((None, ts, DT_COLS), tile),
            pl.BlockSpec((ts // CHUNK, DT_COLS, CHUNK), lambda b, s: (b * ns + s, 0, 0)),
            pl.BlockSpec((3, C_CONV_CH), const),
            pl.BlockSpec((1, C_CONV_CH), const),
            pl.BlockSpec((1, DT_COLS), const),
            pl.BlockSpec((DT_COLS, 1), const),
            pl.BlockSpec((1, DT_COLS), const),
            pl.BlockSpec((DT_COLS, 1), const),
            pl.BlockSpec((1, C_WIDTH), const),
        ],
        out_specs=[
            pl.BlockSpec((None, ts, C_CONV_CH), tile),
            pl.BlockSpec((None, ts, C_WIDTH), tile),
        ],
        out_shape=[
            jax.ShapeDtypeStruct((Bn, S, C_CONV_CH), BF16),
            jax.ShapeDtypeStruct((Bn, S, C_WIDTH), BF16),
        ],
        scratch_shapes=[pltpu.VMEM((C_STATE, C_WIDTH), F32)],
        compiler_params=pltpu.CompilerParams(
            dimension_semantics=("arbitrary", "arbitrary"), vmem_limit_bytes=VMEM_LIMIT_BYTES),
        name="ssd_fwd",
    )(xbc3, xbc3, xbc3, dt3, dtT, conv_w, conv_b, brow, bcol, arow, acol, dskip)


def _mix_out_kernel(xact_ref, dt_ref, dtT_ref, yf_ref, zc_ref, zab_ref,
                    o1_ref, o4_ref, o16_ref, l1_ref, l4_ref, l16_ref, h_ref,
                    brow_ref, bcol_ref, arow_ref, acol_ref,
                    sguw_ref, sgub_ref, e8_ref, nrm_ref, wout_ref, fnw_ref,
                    out_ref, state_ref, ybuf_ref, mixed_ref, *, ts, final):
    s = pl.program_id(1)

    @pl.when(s == 0)
    def _():
        state_ref[...] = jnp.zeros_like(state_ref)

    brow, bcol = brow_ref[:, C_HEADS:DT_COLS], bcol_ref[C_HEADS:DT_COLS, :]
    arow, acol = arow_ref[:, C_HEADS:DT_COLS], acol_ref[C_HEADS:DT_COLS, :]
    nch = ts // CHUNK

    def body(i, carry):
        c = nch - 1 - i
        r0 = pl.multiple_of(c * CHUNK, CHUNK)
        rows = pl.ds(r0, CHUNK)

        xs = xact_ref[rows, 0:C_WIDTH]
        bm = xact_ref[rows, C_WIDTH:C_WIDTH + C_GROUPS * C_STATE]
        cm = xact_ref[rows, C_WIDTH + C_GROUPS * C_STATE:C_CONV_CH]
        ys = _ssd_chunk(xs, bm, cm, dt_ref[rows, C_HEADS:DT_COLS], dtT_ref[c, C_HEADS:DT_COLS, :],
                        brow, bcol, arow, acol, state_ref, reverse=True)
        ssq = jnp.zeros((CHUNK, 1), F32)
        for pair, y in enumerate(ys):
            cols = slice(pair * LANES, (pair + 1) * LANES)
            y = (y + yf_ref[rows, cols].astype(F32)) * _silu(zc_ref[rows, cols].astype(F32))
            ybuf_ref[:, cols] = y
            ssq = ssq + jnp.sum(y * y, axis=-1, keepdims=True)
        inv = lax.rsqrt(ssq * (1.0 / C_WIDTH) + EPS)
        mixed_ref[rows, A_WIDTH + B_WIDTH:D_MIX] = (ybuf_ref[...] * inv * nrm_ref[...]).astype(BF16)

        l1, l4, l16 = l1_ref[rows, :], l4_ref[rows, :], l16_ref[rows, :]
        mx = jnp.maximum(jnp.maximum(l1, l4), l16)
        e1, e4, e16 = jnp.exp(l1 - mx), jnp.exp(l4 - mx), jnp.exp(l16 - mx)
        den = e1 + e4 + e16
        oa = jnp.zeros((CHUNK, A_WIDTH), F32)
        for ec, o_ref in ((e1, o1_ref), (e4, o4_ref), (e16, o16_ref)):
            wexp = _dot((ec / den).astype(BF16), e8_ref[...])
            oa = oa + wexp * o_ref[rows, :].astype(F32)
        za = zab_ref[rows, 0:A_WIDTH].astype(F32)
        mixed_ref[rows, 0:A_WIDTH] = (oa * _silu(za)).astype(BF16)

        for g in range(B_GROUPS):
            cols = slice(g * LANES, (g + 1) * LANES)
            ub = zab_ref[rows, A_WIDTH + g * LANES:A_WIDTH + (g + 1) * LANES].astype(F32)
            vb = zab_ref[rows, 2 * A_WIDTH + g * LANES:2 * A_WIDTH + (g + 1) * LANES]
            zb = zab_ref[rows, 3 * A_WIDTH + g * LANES:3 * A_WIDTH + (g + 1) * LANES].astype(F32)
            mixed = _dot(sguw_ref[g], vb) + sgub_ref[:, cols]
            mixed_ref[rows, A_WIDTH + g * LANES:A_WIDTH + (g + 1) * LANES] = (
                ub * mixed * _silu(zb)).astype(BF16)
        return carry

    lax.fori_loop(0, nch, body, 0)

    hn = h_ref[...] + _dot(mixed_ref[...], wout_ref[...])
    if final:
        ms = jnp.mean(hn * hn, axis=-1, keepdims=True)
        hn = hn * lax.rsqrt(ms + EPS) * fnw_ref[...]
    out_ref[...] = hn


def _mix_out(xact3, dt3, dtT, yf3, zc3, zab3, o1, o4, o16, l1, l4, l16, h3,
             brow, bcol, arow, acol, sguw, sgub, e8, nrm, wout, fnw, final):
    Bn, S, _ = xact3.shape
    ts = TS_SEQ
    ns = S // ts
    const2 = lambda b, s: (0, 0)
    const3 = lambda b, s: (0, 0, 0)
    tile = lambda b, s: (b, ns - 1 - s, 0)
    tspec = lambda w: pl.BlockSpec((None, ts, w), tile)
    return pl.pallas_call(
        functools.partial(_mix_out_kernel, ts=ts, final=final),
        grid=(Bn, ns),
        in_specs=[
            tspec(C_CONV_CH), tspec(DT_COLS),
            pl.BlockSpec((ts // CHUNK, DT_COLS, CHUNK), lambda b, s: (b * ns + ns - 1 - s, 0, 0)),
            tspec(C_WIDTH), tspec(C_WIDTH), tspec(2048),
            tspec(A_WIDTH), tspec(A_WIDTH), tspec(A_WIDTH),
            tspec(LANES), tspec(LANES), tspec(LANES),
            tspec(D_MODEL),
            pl.BlockSpec((1, DT_COLS), const2),
            pl.BlockSpec((DT_COLS, 1), const2),
            pl.BlockSpec((1, DT_COLS), const2),
            pl.BlockSpec((DT_COLS, 1), const2),
            pl.BlockSpec((B_GROUPS, CHUNK, CHUNK), const3),
            pl.BlockSpec((CHUNK, B_WIDTH), const2),
            pl.BlockSpec((LANES, A_WIDTH), const2),
            pl.BlockSpec((1, C_WIDTH), const2),
            pl.BlockSpec((D_MIX, D_MODEL), const2),
            pl.BlockSpec((1, D_MODEL), const2),
        ],
        out_specs=pl.BlockSpec((None, ts, D_MODEL), tile),
        out_shape=jax.ShapeDtypeStruct((Bn, S, D_MODEL), F32),
        scratch_shapes=[
            pltpu.VMEM((C_STATE, C_WIDTH), F32),
            pltpu.VMEM((CHUNK, C_WIDTH), F32),
            pltpu.VMEM((ts, D_MIX), BF16),
        ],
        compiler_params=pltpu.CompilerParams(
            dimension_semantics=("arbitrary", "arbitrary"), vmem_limit_bytes=VMEM_LIMIT_BYTES),
        name="mix_out_final" if final else "mix_out",
    )(xact3, dt3, dtT, yf3, zc3, zab3, o1, o4, o16, l1, l4, l16, h3,
      brow, bcol, arow, acol, sguw, sgub, e8, nrm, wout, fnw)


def _rope_tables(S):
    inv = ROPE_THETA ** (-jnp.arange(0, HEAD_DIM, 2, dtype=F32) / HEAD_DIM)
    ang = jnp.arange(S, dtype=F32)[:, None] * inv[None, :]
    cos, sin = jnp.cos(ang), jnp.sin(ang)
    cos_t = jnp.concatenate([cos, cos, cos, cos], axis=-1)
    sin_t = jnp.concatenate([-sin, sin, -sin, sin], axis=-1)
    return cos_t, sin_t


def _layer_params(l, norm_w, w_in, sgu_w, sgu_b, conv_w, conv_b, dt_bias, a_log, d_skip,
                  ssd_norm_w, w_out):
    w = w_in[l]
    head_of_lane = jnp.arange(A_WIDTH) // HEAD_DIM
    e8 = (jnp.arange(LANES)[:, None] == head_of_lane[None, :]).astype(BF16)
    return dict(
        nw=norm_w[l][None, :],
        w_main=w[:, :MAIN_COLS].astype(BF16),
        w_dt=w[:, MAIN_COLS:].astype(BF16),
        w_dtT=w[:, MAIN_COLS:].T.astype(BF16),
        sguw=sgu_w[l].astype(BF16),
        sgub=jnp.repeat(sgu_b[l].T, B_WIDTH // B_GROUPS, axis=1),
        conv_w=conv_w[l],
        conv_b=conv_b[l][None, :],
        brow=dt_bias[l].reshape(1, DT_COLS),
        bcol=dt_bias[l].reshape(DT_COLS, 1),
        arow=a_log[l].reshape(1, DT_COLS),
        acol=a_log[l].reshape(DT_COLS, 1),
        dskip=jnp.repeat(d_skip[l], HEAD_DIM)[None, :],
        nrm=ssd_norm_w[l][None, :],
        wout=w_out[l].astype(BF16),
        e8=e8,
    )


def _trunk(x, layers, fnw):
    Bn, S, _ = x.shape
    T = Bn * S
    assert S % (ATT_Q * max(d for _, d in ATT_CFGS)) == 0 and S % TS_SEQ == 0 and S % TM_PROJ == 0
    cos_t, sin_t = _rope_tables(S)
    h = x
    for li, p in enumerate(layers):
        qkv, zab, xbc, zc, dt, dtT = _in_proj(h.reshape(T, D_MODEL), S, p["nw"], cos_t, sin_t,
                                              p["w_main"], p["w_dt"], p["w_dtT"])
        qkv3 = qkv.reshape(Bn, S, 3 * A_WIDTH)
        att = [_attention(qkv3, dil) for _, dil in ATT_CFGS]
        xbc3 = xbc.reshape(Bn, S, C_CONV_CH)
        dt3 = dt.reshape(Bn, S, DT_COLS)
        xact3, yf3 = _ssd_fwd(xbc3, dt3, dtT, p["conv_w"], p["conv_b"], p["brow"], p["bcol"],
                              p["arow"], p["acol"], p["dskip"])
        h = _mix_out(xact3, dt3, dtT, yf3, zc.reshape(Bn, S, C_WIDTH), zab.reshape(Bn, S, 2048),
                     att[0][0], att[1][0], att[2][0], att[0][1], att[1][1], att[2][1], h,
                     p["brow"], p["bcol"], p["arow"], p["acol"], p["sguw"], p["sgub"], p["e8"],
                     p["nrm"], p["wout"], fnw, final=(li == len(layers) - 1))
    return h


def kernel(x_prompt, x_sample, norm_w, w_in, sgu_w, sgu_b, conv_w, conv_b, dt_bias, a_log, d_skip,
           ssd_norm_w, w_out, final_norm_w):
    depth = w_in.shape[0]
    layers = [_layer_params(l, norm_w, w_in, sgu_w, sgu_b, conv_w, conv_b, dt_bias, a_log, d_skip,
                            ssd_norm_w, w_out) for l in range(depth)]
    fnw = final_norm_w[None, :]
    return (_trunk(x_prompt, layers, fnw), _trunk(x_sample, layers, fnw))
```

```python
import functools

import jax
import jax.numpy as jnp
from jax import lax
from jax.experimental import pallas as pl
from jax.experimental.pallas import tpu as pltpu

F32 = jnp.float32
BF16 = jnp.bfloat16

D_MODEL = 1024
HEAD_DIM = 64
A_WIDTH = 512
A_HEADS = 8
ATT_CFGS = ((128, 1), (512, 4), (2048, 16))
ATT_Q = 128
ATT_HALF = 64
ROPE_THETA = 10000.0
B_WIDTH = 512
B_GROUPS = 4
CHUNK = 128
C_WIDTH = 1024
C_HEADS = 16
C_STATE = 128
C_GROUPS = 2
C_CONV_CH = C_WIDTH + 2 * C_GROUPS * C_STATE
MAIN_COLS = 6144
DT_COLS = 2 * C_HEADS
D_MIX = 2048
EPS = 1e-5
NEG = -1e30
LOG2E = 1.4426950408889634
LN2 = 0.6931471805599453

VMEM_LIMIT_BYTES = 56 * 1024 * 1024
LANES = 128
BF16_SUBLANES = 16

TM_PROJ = 512
TS_SEQ = 512
ATT_QB = 512


def _silu(x):
    return x / (1.0 + jnp.exp(-x))


def _softplus(x):
    return jnp.maximum(x, 0.0) + jnp.log1p(jnp.exp(-jnp.abs(x)))


def _split3(x):
    hi = x.astype(BF16)
    r1 = x - hi.astype(F32)
    mid = r1.astype(BF16)
    lo = (r1 - mid.astype(F32)).astype(BF16)
    return hi, mid, lo


def _dot(a, b):
    return jnp.dot(a, b, preferred_element_type=F32)


def _dot_nt(a, b):
    return lax.dot_general(a, b, (((1,), (1,)), ((), ())), preferred_element_type=F32)


def _inproj_kernel(x_ref, nw_ref, cos_ref, sin_ref, w_ref, wdt_ref, wdtT_ref,
                   qkv_ref, zab_ref, xbc_ref, zc_ref, dt_ref, dtT_ref):
    x = x_ref[...]
    ms = jnp.mean(x * x, axis=-1, keepdims=True)
    xn = (x * lax.rsqrt(ms + EPS) * nw_ref[...]).astype(BF16)
    cos = cos_ref[...]
    sin = sin_ref[...]
    lane = lax.broadcasted_iota(jnp.int32, cos.shape, 1)
    first_half = (lane % HEAD_DIM) < (HEAD_DIM // 2)

    def rope_group(col0, scale):
        acc = _dot(xn, w_ref[:, col0:col0 + A_WIDTH])
        for j in range(A_WIDTH // LANES):
            t = acc[:, j * LANES:(j + 1) * LANES]
            fwd = pltpu.roll(t, HEAD_DIM // 2, 1)
            bwd = pltpu.roll(t, LANES - HEAD_DIM // 2, 1)
            rot = jnp.where(first_half, bwd, fwd)
            out = t * cos + rot * sin
            if scale != 1.0:
                out = out * scale
            qkv_ref[:, col0 + j * LANES:col0 + (j + 1) * LANES] = out.astype(BF16)

    rope_group(0, HEAD_DIM ** -0.5 * LOG2E)
    rope_group(A_WIDTH, 1.0)
    qkv_ref[:, 2 * A_WIDTH:3 * A_WIDTH] = _dot(xn, w_ref[:, 2 * A_WIDTH:3 * A_WIDTH]).astype(BF16)
    base = 3 * A_WIDTH
    for j in range(4):
        zab_ref[:, j * 512:(j + 1) * 512] = _dot(xn, w_ref[:, base + j * 512:base + (j + 1) * 512]).astype(BF16)
    base += 2048
    for j in range(3):
        xbc_ref[:, j * 512:(j + 1) * 512] = _dot(xn, w_ref[:, base + j * 512:base + (j + 1) * 512]).astype(BF16)
    base += C_CONV_CH
    for j in range(2):
        zc_ref[:, j * 512:(j + 1) * 512] = _dot(xn, w_ref[:, base + j * 512:base + (j + 1) * 512]).astype(BF16)
    dt_ref[...] = _dot(xn, wdt_ref[...])
    dtT = _dot_nt(wdtT_ref[...], xn)
    for c in range(dtT_ref.shape[0]):
        dtT_ref[c] = dtT[:, c * CHUNK:(c + 1) * CHUNK]


def _in_proj(x2d, seq_len, nw, cos_t, sin_t, w_main, w_dt, w_dtT):
    T = x2d.shape[0]
    tm = TM_PROJ
    tiles_per_seq = seq_len // tm
    const = lambda i: (0, 0)
    row = lambda i: (i, 0)
    return pl.pallas_call(
        _inproj_kernel,
        grid=(T // tm,),
        in_specs=[
            pl.BlockSpec((tm, D_MODEL), row),
            pl.BlockSpec((1, D_MODEL), const),
            pl.BlockSpec((tm, LANES), lambda i: (i % tiles_per_seq, 0)),
            pl.BlockSpec((tm, LANES), lambda i: (i % tiles_per_seq, 0)),
            pl.BlockSpec((D_MODEL, MAIN_COLS), const),
            pl.BlockSpec((D_MODEL, DT_COLS), const),
            pl.BlockSpec((DT_COLS, D_MODEL), const),
        ],
        out_specs=[
            pl.BlockSpec((tm, 3 * A_WIDTH), row),
            pl.BlockSpec((tm, 2048), row),
            pl.BlockSpec((tm, C_CONV_CH), row),
            pl.BlockSpec((tm, C_WIDTH), row),
            pl.BlockSpec((tm, DT_COLS), row),
            pl.BlockSpec((tm // CHUNK, DT_COLS, CHUNK), lambda i: (i, 0, 0)),
        ],
        out_shape=[
            jax.ShapeDtypeStruct((T, 3 * A_WIDTH), BF16),
            jax.ShapeDtypeStruct((T, 2048), BF16),
            jax.ShapeDtypeStruct((T, C_CONV_CH), BF16),
            jax.ShapeDtypeStruct((T, C_WIDTH), BF16),
            jax.ShapeDtypeStruct((T, DT_COLS), F32),
            jax.ShapeDtypeStruct((T // CHUNK, DT_COLS, CHUNK), F32),
        ],
        compiler_params=pltpu.CompilerParams(
            dimension_semantics=("arbitrary",), vmem_limit_bytes=VMEM_LIMIT_BYTES),
        name="in_proj",
    )(x2d, nw, cos_t, sin_t, w_main, w_dt, w_dtT)


def _attn_kernel(q_ref, kp_ref, kc_ref, kn_ref, vp_ref, vc_ref, vn_ref, o_ref, lse_ref,
                 kbuf, vbuf, *, qb, sub_len):
    n = pl.program_id(2)
    kbuf[0:ATT_Q, :] = kp_ref[...]
    kbuf[ATT_Q:ATT_Q + qb, :] = kc_ref[...]
    kbuf[ATT_Q + qb:, :] = kn_ref[...]
    vbuf[0:ATT_Q, :] = vp_ref[...]
    vbuf[ATT_Q:ATT_Q + qb, :] = vc_ref[...]
    vbuf[ATT_Q + qb:, :] = vn_ref[...]

    nkeys = ATT_Q + 2 * ATT_HALF
    qq = lax.broadcasted_iota(jnp.int32, (ATT_Q, nkeys), 0)
    kk = lax.broadcasted_iota(jnp.int32, (ATT_Q, nkeys), 1)
    kk1 = lax.broadcasted_iota(jnp.int32, (1, nkeys), 1)
    lane = lax.broadcasted_iota(jnp.int32, (ATT_Q, LANES), 1)
    low_half = lane < HEAD_DIM
    high_half = jnp.logical_not(low_half)
    band_bias = jnp.where((kk >= qq) & (kk <= qq + 2 * ATT_HALF), 0.0, NEG)

    for i in range(qb // ATT_Q):
        r0 = i * ATT_Q
        k0 = i * ATT_Q + ATT_Q - ATT_HALF
        q = q_ref[r0:r0 + ATT_Q, :]
        kw = kbuf[k0:k0 + nkeys, :]
        vw = vbuf[k0:k0 + nkeys, :]
        pos = kk1 + (n * qb + i * ATT_Q - ATT_HALF)
        bias = band_bias + jnp.where((pos >= 0) & (pos < sub_len), 0.0, NEG)
        stats = jnp.zeros((ATT_Q, LANES), F32)
        for j in range(A_WIDTH // LANES):
            cols = slice(j * LANES, (j + 1) * LANES)
            qp, kp, vp = q[:, cols], kw[:, cols], vw[:, cols]
            zero = jnp.zeros_like(qp)
            qs = jnp.concatenate([jnp.where(low_half, qp, zero), jnp.where(high_half, qp, zero)], axis=0)
            s = _dot_nt(qs, kp)
            ps = []
            for e in range(2):
                se = s[e * ATT_Q:(e + 1) * ATT_Q, :] + bias
                m = jnp.max(se, axis=-1, keepdims=True)
                p = jnp.exp2(se - m)
                den = jnp.sum(p, axis=-1, keepdims=True)
                ps.append(p.astype(BF16))
                stats = jnp.where(lane == 2 * j + e, m, stats)
                stats = jnp.where(lane == A_HEADS + 2 * j + e, den, stats)
            o = _dot(jnp.concatenate(ps, axis=0), vp)
            o_ref[r0:r0 + ATT_Q, cols] = jnp.where(low_half, o[0:ATT_Q, :], o[ATT_Q:, :]).astype(BF16)
        lse_ref[r0:r0 + ATT_Q, :] = stats


def _attention(qkv3, dil):
    Bn, S, _ = qkv3.shape
    L = S // dil
    qb = min(ATT_QB, L)
    nb = L // qb
    r128 = qb // ATT_Q
    last128 = L // ATT_Q - 1
    view = qkv3.reshape(Bn, L, dil * 3 * A_WIDTH)

    def cur(c):
        return pl.BlockSpec((None, qb, A_WIDTH), lambda b, r, n: (b, n, 3 * r + c))

    def prev(c):
        return pl.BlockSpec((None, ATT_Q, A_WIDTH),
                            lambda b, r, n: (b, jnp.maximum(n * r128 - 1, 0), 3 * r + c))

    def nxt(c):
        return pl.BlockSpec((None, ATT_Q, A_WIDTH),
                            lambda b, r, n: (b, jnp.minimum((n + 1) * r128, last128), 3 * r + c))

    o, lse = pl.pallas_call(
        functools.partial(_attn_kernel, qb=qb, sub_len=L),
        grid=(Bn, dil, nb),
        in_specs=[cur(0), prev(1), cur(1), nxt(1), prev(2), cur(2), nxt(2)],
        out_specs=[
            pl.BlockSpec((None, qb, A_WIDTH), lambda b, r, n: (b, n, r)),
            pl.BlockSpec((None, qb, LANES), lambda b, r, n: (b, n, r)),
        ],
        out_shape=[
            jax.ShapeDtypeStruct((Bn, L, dil * A_WIDTH), BF16),
            jax.ShapeDtypeStruct((Bn, L, dil * LANES), F32),
        ],
        scratch_shapes=[
            pltpu.VMEM((qb + 2 * ATT_Q, A_WIDTH), BF16),
            pltpu.VMEM((qb + 2 * ATT_Q, A_WIDTH), BF16),
        ],
        compiler_params=pltpu.CompilerParams(
            dimension_semantics=("arbitrary", "arbitrary", "arbitrary"),
            vmem_limit_bytes=VMEM_LIMIT_BYTES),
        name=f"attn_d{dil}",
    )(view, view, view, view, view, view, view)
    return o.reshape(Bn, S, A_WIDTH), lse.reshape(Bn, S, LANES)


def _ssd_chunk(xs, bm, cm, dt_raw, dt_rawT, bias_row, bias_col, alog_row, alog_col,
               state_ref, reverse):
    L = CHUNK
    dt = _softplus(dt_raw + bias_row)
    dtT = _softplus(dt_rawT + bias_col)
    dtA = dt * (-jnp.exp(alog_row))
    dtAT = dtT * (-jnp.exp(alog_col))
    ri = lax.broadcasted_iota(jnp.int32, (L, L), 0)
    ci = lax.broadcasted_iota(jnp.int32, (L, L), 1)
    tri = (ci >= ri) if reverse else (ci <= ri)
    tri_bf = jnp.where(tri, 1.0, 0.0).astype(BF16)
    acum = sum(_dot(tri_bf, t) for t in _split3(dtA))
    acumT = sum(_dot_nt(t, tri_bf) for t in _split3(dtAT))
    e = 0 if reverse else L - 1
    edge_row = acum[e:e + 1, :]
    edge_col = acumT[:, e:e + 1]
    wT = jnp.exp(edge_col - acumT) * dtT
    cdec = jnp.exp(edge_row)
    a2 = acum * LOG2E
    srcT = acumT * LOG2E - jnp.log2(dtT)
    wT_bf = wT.astype(BF16)
    lane = lax.broadcasted_iota(jnp.int32, (L, LANES), 1)
    low_half = lane < HEAD_DIM
    high_half = jnp.logical_not(low_half)

    def block_diag(t):
        zero = jnp.zeros_like(t)
        return jnp.concatenate([jnp.where(low_half, t, zero), jnp.where(high_half, t, zero)], axis=0)

    ys = []
    for g in range(C_GROUPS):
        bg = bm[:, g * C_STATE:(g + 1) * C_STATE]
        cg = cm[:, g * C_STATE:(g + 1) * C_STATE]
        cb = _dot_nt(cg, bg).astype(BF16)
        bgT = bg.astype(F32).T.astype(BF16)
        for pr in range(4):
            pair = g * 4 + pr
            cols = slice(pair * LANES, (pair + 1) * LANES)
            xbd = block_diag(xs[:, cols])
            prev = state_ref[:, cols]
            pbd = block_diag(prev.astype(BF16))
            ws, css, bss, cd = [], [], [], []
            for k in range(2):
                h = 2 * pair + k
                col = jnp.broadcast_to(a2[:, h:h + 1], (L, L))
                dec = jnp.exp2(jnp.where(tri, col - srcT[h:h + 1, :], NEG))
                ws.append(cb * dec.astype(BF16))
                css.append(cg * jnp.exp2(col).astype(BF16))
                bss.append(bgT * jnp.broadcast_to(wT_bf[h:h + 1, :], (C_STATE, L)))
                cd.append(cdec[:, h:h + 1])
            ys.append(_dot(jnp.concatenate(ws + css, axis=1), jnp.concatenate([xbd, pbd], axis=0)))
            decay = jnp.where(low_half[0:1, :], cd[0], cd[1])
            state_ref[:, cols] = prev * decay + _dot(jnp.concatenate(bss, axis=1), xbd)
    return ys


def _ssd_fwd_kernel(xbc_ref, hp_ref, hn_ref, dt_ref, dtT_ref, cw_ref, cb_ref,
                    brow_ref, bcol_ref, arow_ref, acol_ref, dskip_ref,
                    xact_ref, yf_ref, state_ref, *, ts):
    s = pl.program_id(1)
    ns = pl.num_programs(1)

    @pl.when(s == 0)
    def _():
        state_ref[...] = jnp.zeros_like(state_ref)

    x = xbc_ref[...].astype(F32)
    prev_row = jnp.where(s > 0, hp_ref[BF16_SUBLANES - 1:BF16_SUBLANES, :].astype(F32), 0.0)
    next_row = jnp.where(s < ns - 1, hn_ref[0:1, :].astype(F32), 0.0)
    row = lax.broadcasted_iota(jnp.int32, x.shape, 0)
    xm1 = jnp.where(row == 0, prev_row, pltpu.roll(x, 1, 0))
    xp1 = jnp.where(row == ts - 1, next_row, pltpu.roll(x, ts - 1, 0))
    conv = cw_ref[0:1, :] * xm1 + cw_ref[1:2, :] * x + cw_ref[2:3, :] * xp1 + cb_ref[...]
    xact_ref[...] = _silu(conv).astype(BF16)

    brow, bcol = brow_ref[:, 0:C_HEADS], bcol_ref[0:C_HEADS, :]
    arow, acol = arow_ref[:, 0:C_HEADS], acol_ref[0:C_HEADS, :]

    def body(c, carry):
        r0 = pl.multiple_of(c * CHUNK, CHUNK)
        rows = pl.ds(r0, CHUNK)
        xs = xact_ref[rows, 0:C_WIDTH]
        bm = xact_ref[rows, C_WIDTH:C_WIDTH + C_GROUPS * C_STATE]
        cm = xact_ref[rows, C_WIDTH + C_GROUPS * C_STATE:C_CONV_CH]
        ys = _ssd_chunk(xs, bm, cm, dt_ref[rows, 0:C_HEADS], dtT_ref[c, 0:C_HEADS, :],
                        brow, bcol, arow, acol, state_ref, reverse=False)
        for pair, y in enumerate(ys):
            cols = slice(pair * LANES, (pair + 1) * LANES)
            y = y + dskip_ref[:, cols] * xs[:, cols].astype(F32)
            yf_ref[rows, cols] = y.astype(BF16)
        return carry

    lax.fori_loop(0, ts // CHUNK, body, 0)


def _ssd_fwd(xbc3, dt3, dtT, conv_w, conv_b, brow, bcol, arow, acol, dskip):
    Bn, S, _ = xbc3.shape
    ts = TS_SEQ
    ns = S // ts
    hb = ts // BF16_SUBLANES
    last_hb = S // BF16_SUBLANES - 1
    const = lambda b, s: (0, 0)
    tile = lambda b, s: (b, s, 0)
    return pl.pallas_call(
        functools.partial(_ssd_fwd_kernel, ts=ts),
        grid=(Bn, ns),
        in_specs=[
            pl.BlockSpec((None, ts, C_CONV_CH), tile),
            pl.BlockSpec((None, BF16_SUBLANES, C_CONV_CH),
                         lambda b, s: (b, jnp.maximum(s * hb - 1, 0), 0)),
            pl.BlockSpec((None, BF16_SUBLANES, C_CONV_CH),
                         lambda b, s: (b, jnp.minimum((s + 1) * hb, last_hb), 0)),
            pl.BlockSpec((None, ts, DT_COLS), tile),
            pl.BlockSpec((ts // CHUNK, DT_COLS, CHUNK), lambda b, s: (b * ns + s, 0, 0)),
            pl.BlockSpec((3, C_CONV_CH), const),
            pl.BlockSpec((1, C_CONV_CH), const),
            pl.BlockSpec((1, DT_COLS), const),
            pl.BlockSpec((DT_COLS, 1), const),
            pl.BlockSpec((1, DT_COLS), const),
            pl.BlockSpec((DT_COLS, 1), const),
            pl.BlockSpec((1, C_WIDTH), const),
        ],
        out_specs=[
            pl.BlockSpec((None, ts, C_CONV_CH), tile),
            pl.BlockSpec((None, ts, C_WIDTH), tile),
        ],
        out_shape=[
            jax.ShapeDtypeStruct((Bn, S, C_CONV_CH), BF16),
            jax.ShapeDtypeStruct((Bn, S, C_WIDTH), BF16),
        ],
        scratch_shapes=[pltpu.VMEM((C_STATE, C_WIDTH), F32)],
        compiler_params=pltpu.CompilerParams(
            dimension_semantics=("arbitrary", "arbitrary"), vmem_limit_bytes=VMEM_LIMIT_BYTES),
        name="ssd_fwd",
    )(xbc3, xbc3, xbc3, dt3, dtT, conv_w, conv_b, brow, bcol, arow, acol, dskip)


def _mix_out_kernel(xact_ref, dt_ref, dtT_ref, yf_ref, zc_ref, zab_ref,
                    o1_ref, o4_ref, o16_ref, l1_ref, l4_ref, l16_ref, h_ref,
                    brow_ref, bcol_ref, arow_ref, acol_ref,
                    sguw_ref, sgub_ref, e8_ref, nrm_ref, wout_ref, fnw_ref,
                    out_ref, state_ref, ybuf_ref, mixed_ref, *, ts, final):
    s = pl.program_id(1)

    @pl.when(s == 0)
    def _():
        state_ref[...] = jnp.zeros_like(state_ref)

    brow, bcol = brow_ref[:, C_HEADS:DT_COLS], bcol_ref[C_HEADS:DT_COLS, :]
    arow, acol = arow_ref[:, C_HEADS:DT_COLS], acol_ref[C_HEADS:DT_COLS, :]
    nch = ts // CHUNK

    def body(i, carry):
        c = nch - 1 - i
        r0 = pl.multiple_of(c * CHUNK, CHUNK)
        rows = pl.ds(r0, CHUNK)

        xs = xact_ref[rows, 0:C_WIDTH]
        bm = xact_ref[rows, C_WIDTH:C_WIDTH + C_GROUPS * C_STATE]
        cm = xact_ref[rows, C_WIDTH + C_GROUPS * C_STATE:C_CONV_CH]
        ys = _ssd_chunk(xs, bm, cm, dt_ref[rows, C_HEADS:DT_COLS], dtT_ref[c, C_HEADS:DT_COLS, :],
                        brow, bcol, arow, acol, state_ref, reverse=True)
        ssq = jnp.zeros((CHUNK, 1), F32)
        for pair, y in enumerate(ys):
            cols = slice(pair * LANES, (pair + 1) * LANES)
            y = (y + yf_ref[rows, cols].astype(F32)) * _silu(zc_ref[rows, cols].astype(F32))
            ybuf_ref[:, cols] = y
            ssq = ssq + jnp.sum(y * y, axis=-1, keepdims=True)
        inv = lax.rsqrt(ssq * (1.0 / C_WIDTH) + EPS)
        mixed_ref[rows, A_WIDTH + B_WIDTH:D_MIX] = (ybuf_ref[...] * inv * nrm_ref[...]).astype(BF16)

        st = [r[rows, :] for r in (l1_ref, l4_ref, l16_ref)]
        mx = jnp.maximum(jnp.maximum(st[0], st[1]), st[2])
        es = [jnp.exp2(t - mx) for t in st]
        dens = [pltpu.roll(t, LANES - A_HEADS, 1) for t in st]
        z = es[0] * dens[0] + es[1] * dens[1] + es[2] * dens[2]
        head_lane = lax.broadcasted_iota(jnp.int32, (CHUNK, LANES), 1) < A_HEADS
        rz = 1.0 / jnp.where(head_lane, z, 1.0)
        oa = jnp.zeros((CHUNK, A_WIDTH), F32)
        for ec, o_ref in zip(es, (o1_ref, o4_ref, o16_ref)):
            wexp = _dot((ec * rz).astype(BF16), e8_ref[...])
            oa = oa + wexp * o_ref[rows, :].astype(F32)
        za = zab_ref[rows, 0:A_WIDTH].astype(F32)
        mixed_ref[rows, 0:A_WIDTH] = (oa * _silu(za)).astype(BF16)

        for g in range(B_GROUPS):
            cols = slice(g * LANES, (g + 1) * LANES)
            ub = zab_ref[rows, A_WIDTH + g * LANES:A_WIDTH + (g + 1) * LANES].astype(F32)
            vb = zab_ref[rows, 2 * A_WIDTH + g * LANES:2 * A_WIDTH + (g + 1) * LANES]
            zb = zab_ref[rows, 3 * A_WIDTH + g * LANES:3 * A_WIDTH + (g + 1) * LANES].astype(F32)
            mixed = _dot(sguw_ref[g], vb) + sgub_ref[:, cols]
            mixed_ref[rows, A_WIDTH + g * LANES:A_WIDTH + (g + 1) * LANES] = (
                ub * mixed * _silu(zb)).astype(BF16)
        return carry

    lax.fori_loop(0, nch, body, 0)

    hn = h_ref[...] + _dot(mixed_ref[...], wout_ref[...])
    if final:
        ms = jnp.mean(hn * hn, axis=-1, keepdims=True)
        hn = hn * lax.rsqrt(ms + EPS) * fnw_ref[...]
    out_ref[...] = hn


def _mix_out(xact3, dt3, dtT, yf3, zc3, zab3, o1, o4, o16, l1, l4, l16, h3,
             brow, bcol, arow, acol, sguw, sgub, e8, nrm, wout, fnw, final):
    Bn, S, _ = xact3.shape
    ts = TS_SEQ
    ns = S // ts
    const2 = lambda b, s: (0, 0)
    const3 = lambda b, s: (0, 0, 0)
    tile = lambda b, s: (b, ns - 1 - s, 0)
    tspec = lambda w: pl.BlockSpec((None, ts, w), tile)
    return pl.pallas_call(
        functools.partial(_mix_out_kernel, ts=ts, final=final),
        grid=(Bn, ns),
        in_specs=[
            tspec(C_CONV_CH), tspec(DT_COLS),
            pl.BlockSpec((ts // CHUNK, DT_COLS, CHUNK), lambda b, s: (b * ns + ns - 1 - s, 0, 0)),
            tspec(C_WIDTH), tspec(C_WIDTH), tspec(2048),
            tspec(A_WIDTH), tspec(A_WIDTH), tspec(A_WIDTH),
            tspec(LANES), tspec(LANES), tspec(LANES),
            tspec(D_MODEL),
            pl.BlockSpec((1, DT_COLS), const2),
            pl.BlockSpec((DT_COLS, 1), const2),
            pl.BlockSpec((1, DT_COLS), const2),
            pl.BlockSpec((DT_COLS, 1), const2),
            pl.BlockSpec((B_GROUPS, CHUNK, CHUNK), const3),
            pl.BlockSpec((CHUNK, B_WIDTH), const2),
            pl.BlockSpec((LANES, A_WIDTH), const2),
            pl.BlockSpec((1, C_WIDTH), const2),
            pl.BlockSpec((D_MIX, D_MODEL), const2),
            pl.BlockSpec((1, D_MODEL), const2),
        ],
        out_specs=pl.BlockSpec((None, ts, D_MODEL), tile),
        out_shape=jax.ShapeDtypeStruct((Bn, S, D_MODEL), F32),
        scratch_shapes=[
            pltpu.VMEM((C_STATE, C_WIDTH), F32),
            pltpu.VMEM((CHUNK, C_WIDTH), F32),
            pltpu.VMEM((ts, D_MIX), BF16),
        ],
        compiler_params=pltpu.CompilerParams(
            dimension_semantics=("arbitrary", "arbitrary"), vmem_limit_bytes=VMEM_LIMIT_BYTES),
        name="mix_out_final" if final else "mix_out",
    )(xact3, dt3, dtT, yf3, zc3, zab3, o1, o4, o16, l1, l4, l16, h3,
      brow, bcol, arow, acol, sguw, sgub, e8, nrm, wout, fnw)


def _rope_tables(S):
    inv = ROPE_THETA ** (-jnp.arange(0, HEAD_DIM, 2, dtype=F32) / HEAD_DIM)
    ang = jnp.arange(S, dtype=F32)[:, None] * inv[None, :]
    cos, sin = jnp.cos(ang), jnp.sin(ang)
    cos_t = jnp.concatenate([cos, cos, cos, cos], axis=-1)
    sin_t = jnp.concatenate([-sin, sin, -sin, sin], axis=-1)
    return cos_t, sin_t


def _layer_params(l, norm_w, w_in, sgu_w, sgu_b, conv_w, conv_b, dt_bias, a_log, d_skip,
                  ssd_norm_w, w_out):
    w = w_in[l]
    head_of_lane = jnp.arange(A_WIDTH) // HEAD_DIM
    e8 = (jnp.arange(LANES)[:, None] == head_of_lane[None, :]).astype(BF16)
    return dict(
        nw=norm_w[l][None, :],
        w_main=w[:, :MAIN_COLS].astype(BF16),
        w_dt=w[:, MAIN_COLS:].astype(BF16),
        w_dtT=w[:, MAIN_COLS:].T.astype(BF16),
        sguw=sgu_w[l].astype(BF16),
        sgub=jnp.repeat(sgu_b[l].T, B_WIDTH // B_GROUPS, axis=1),
        conv_w=conv_w[l],
        conv_b=conv_b[l][None, :],
        brow=dt_bias[l].reshape(1, DT_COLS),
        bcol=dt_bias[l].reshape(DT_COLS, 1),
        arow=a_log[l].reshape(1, DT_COLS),
        acol=a_log[l].reshape(DT_COLS, 1),
        dskip=jnp.repeat(d_skip[l], HEAD_DIM)[None, :],
        nrm=ssd_norm_w[l][None, :],
        wout=w_out[l].astype(BF16),
        e8=e8,
    )


def _trunk(x, layers, fnw):
    Bn, S, _ = x.shape
    T = Bn * S
    assert S % (ATT_Q * max(d for _, d in ATT_CFGS)) == 0 and S % TS_SEQ == 0 and S % TM_PROJ == 0
    cos_t, sin_t = _rope_tables(S)
    h = x
    for li, p in enumerate(layers):
        qkv, zab, xbc, zc, dt, dtT = _in_proj(h.reshape(T, D_MODEL), S, p["nw"], cos_t, sin_t,
                                              p["w_main"], p["w_dt"], p["w_dtT"])
        qkv3 = qkv.reshape(Bn, S, 3 * A_WIDTH)
        att = [_attention(qkv3, dil) for _, dil in ATT_CFGS]
        xbc3 = xbc.reshape(Bn, S, C_CONV_CH)
        dt3 = dt.reshape(Bn, S, DT_COLS)
        xact3, yf3 = _ssd_fwd(xbc3, dt3, dtT, p["conv_w"], p["conv_b"], p["brow"], p["bcol"],
                              p["arow"], p["acol"], p["dskip"])
        h = _mix_out(xact3, dt3, dtT, yf3, zc.reshape(Bn, S, C_WIDTH), zab.reshape(Bn, S, 2048),
                     att[0][0], att[1][0], att[2][0], att[0][1], att[1][1], att[2][1], h,
                     p["brow"], p["bcol"], p["arow"], p["acol"], p["sguw"], p["sgub"], p["e8"],
                     p["nrm"], p["wout"], fnw, final=(li == len(layers) - 1))
    return h


def kernel(x_prompt, x_sample, norm_w, w_in, sgu_w, sgu_b, conv_w, conv_b, dt_bias, a_log, d_skip,
           ssd_norm_w, w_out, final_norm_w):
    depth = w_in.shape[0]
    layers = [_layer_params(l, norm_w, w_in, sgu_w, sgu_b, conv_w, conv_b, dt_bias, a_log, d_skip,
                            ssd_norm_w, w_out) for l in range(depth)]
    fnw = final_norm_w[None, :]
    return (_trunk(x_prompt, layers, fnw), _trunk(x_sample, layers, fnw))
```

```python
import functools

import jax
import jax.numpy as jnp
from jax import lax
from jax.experimental import pallas as pl
from jax.experimental.pallas import tpu as pltpu

F32 = jnp.float32
BF16 = jnp.bfloat16

D_MODEL = 1024
HEAD_DIM = 64
A_WIDTH = 512
A_HEADS = 8
ATT_CFGS = ((128, 1), (512, 4), (2048, 16))
ATT_Q = 128
ATT_HALF = 64
ROPE_THETA = 10000.0
B_WIDTH = 512
B_GROUPS = 4
CHUNK = 128
C_WIDTH = 1024
C_HEADS = 16
C_STATE = 128
C_GROUPS = 2
C_CONV_CH = C_WIDTH + 2 * C_GROUPS * C_STATE
MAIN_COLS = 6144
DT_COLS = 2 * C_HEADS
D_MIX = 2048
EPS = 1e-5
NEG = -1e30
LOG2E = 1.4426950408889634
LN2 = 0.6931471805599453

VMEM_LIMIT_BYTES = 56 * 1024 * 1024
LANES = 128
F32_SUBLANES = 8
MXU_COLS = 256

TM_PROJ = 512
TS_SEQ = 512
ATT_QB = 512


def _silu(x):
    hx = 0.5 * x
    return hx + hx * jnp.tanh(hx)


def _softplus(x):
    return jnp.maximum(x, 0.0) + jnp.log1p(jnp.exp(-jnp.abs(x)))


def _split3(x):
    hi = x.astype(BF16)
    r1 = x - hi.astype(F32)
    mid = r1.astype(BF16)
    lo = (r1 - mid.astype(F32)).astype(BF16)
    return hi, mid, lo


def _dot(a, b):
    return jnp.dot(a, b, preferred_element_type=F32)


def _dot_nt(a, b):
    return lax.dot_general(a, b, (((1,), (1,)), ((), ())), preferred_element_type=F32)


def _inproj_kernel(x_ref, xprev_ref, xnext_ref, nw_ref, cos_ref, sin_ref, w_ref, wdt_ref, wdtT_ref,
                   cw_ref, cb_ref,
                   qkv_ref, qkv4_ref, qkv16_ref, zab_ref, xact_ref, zc_ref, dt_ref, dtT_ref,
                   qkv32_ref, *, tm, tiles_per_seq):
    ti = pl.program_id(0) % tiles_per_seq
    x = jnp.concatenate([x_ref[...], xprev_ref[...], xnext_ref[...]], axis=0)
    ms = jnp.mean(x * x, axis=-1, keepdims=True)
    xn_ext = (x * lax.rsqrt(ms + EPS) * nw_ref[...]).astype(BF16)
    xn = xn_ext[0:tm, :]
    cos = cos_ref[...]
    sin = sin_ref[...]
    lane = lax.broadcasted_iota(jnp.int32, cos.shape, 1)
    first_half = (lane % HEAD_DIM) < (HEAD_DIM // 2)
    gw = MXU_COLS
    gslabs = gw // LANES
    qk_cols, qkv_cols = 2 * A_WIDTH, 3 * A_WIDTH
    zab_end, xbc_end = qkv_cols + 2048, qkv_cols + 2048 + C_CONV_CH
    row = lax.broadcasted_iota(jnp.int32, (tm, gw), 0)

    for c0 in range(0, MAIN_COLS, gw):
        if c0 < qkv_cols:
            acc = _dot(xn, w_ref[:, c0:c0 + gw])
            scale = HEAD_DIM ** -0.5 * LOG2E if c0 < A_WIDTH else 1.0
            for j in range(gslabs):
                slab = c0 // LANES + j
                out = acc[:, j * LANES:(j + 1) * LANES]
                if c0 < qk_cols:
                    fwd = pltpu.roll(out, HEAD_DIM // 2, 1)
                    bwd = pltpu.roll(out, LANES - HEAD_DIM // 2, 1)
                    out = out * cos + jnp.where(first_half, bwd, fwd) * sin
                    if scale != 1.0:
                        out = out * scale
                qkv32_ref[slab] = out
                qkv_ref[:, slab * LANES:(slab + 1) * LANES] = out.astype(BF16)
                for d, dst in ((ATT_CFGS[1][1], qkv4_ref), (ATT_CFGS[2][1], qkv16_ref)):
                    for r in range(d):
                        piece = qkv32_ref[slab, pl.ds(r, tm // d, stride=d), :]
                        dst[:, r * qkv_cols + slab * LANES:r * qkv_cols + (slab + 1) * LANES] = piece.astype(BF16)
        elif c0 < zab_end:
            zab_ref[:, c0 - qkv_cols:c0 - qkv_cols + gw] = _dot(xn, w_ref[:, c0:c0 + gw]).astype(BF16)
        elif c0 < xbc_end:
            cols = slice(c0 - zab_end, c0 - zab_end + gw)
            ext = _dot(xn_ext, w_ref[:, c0:c0 + gw])
            cur = ext[0:tm, :]
            prev_row = jnp.where(ti > 0, ext[tm + 7:tm + 8, :], 0.0)
            next_row = jnp.where(ti < tiles_per_seq - 1, ext[tm + 8:tm + 9, :], 0.0)
            xm1 = jnp.where(row == 0, prev_row, pltpu.roll(cur, 1, 0))
            xp1 = jnp.where(row == tm - 1, next_row, pltpu.roll(cur, tm - 1, 0))
            conv = (cw_ref[0:1, cols] * xm1 + cw_ref[1:2, cols] * cur + cw_ref[2:3, cols] * xp1
                    + cb_ref[:, cols])
            xact_ref[:, cols] = _silu(conv).astype(BF16)
        else:
            zc_ref[:, c0 - xbc_end:c0 - xbc_end + gw] = _dot(xn, w_ref[:, c0:c0 + gw]).astype(BF16)
    dt_ref[...] = _dot(xn, wdt_ref[...])
    dtT = _dot_nt(wdtT_ref[...], xn)
    for c in range(dtT_ref.shape[0]):
        dtT_ref[c] = dtT[:, c * CHUNK:(c + 1) * CHUNK]


def _in_proj(x2d, seq_len, nw, cos_t, sin_t, w_main, w_dt, w_dtT, conv_w, conv_b):
    T = x2d.shape[0]
    tm = TM_PROJ
    tiles_per_seq = seq_len // tm
    hb = tm // F32_SUBLANES
    last_hb = T // F32_SUBLANES - 1
    d4, d16 = ATT_CFGS[1][1], ATT_CFGS[2][1]
    const = lambda i: (0, 0)
    row = lambda i: (i, 0)
    return pl.pallas_call(
        functools.partial(_inproj_kernel, tm=tm, tiles_per_seq=tiles_per_seq),
        grid=(T // tm,),
        in_specs=[
            pl.BlockSpec((tm, D_MODEL), row),
            pl.BlockSpec((F32_SUBLANES, D_MODEL), lambda i: (jnp.maximum(i * hb - 1, 0), 0)),
            pl.BlockSpec((F32_SUBLANES, D_MODEL), lambda i: (jnp.minimum((i + 1) * hb, last_hb), 0)),
            pl.BlockSpec((1, D_MODEL), const),
            pl.BlockSpec((tm, LANES), lambda i: (i % tiles_per_seq, 0)),
            pl.BlockSpec((tm, LANES), lambda i: (i % tiles_per_seq, 0)),
            pl.BlockSpec((D_MODEL, MAIN_COLS), const),
            pl.BlockSpec((D_MODEL, DT_COLS), const),
            pl.BlockSpec((DT_COLS, D_MODEL), const),
            pl.BlockSpec((3, C_CONV_CH), const),
            pl.BlockSpec((1, C_CONV_CH), const),
        ],
        out_specs=[
            pl.BlockSpec((tm, 3 * A_WIDTH), row),
            pl.BlockSpec((tm // d4, d4 * 3 * A_WIDTH), row),
            pl.BlockSpec((tm // d16, d16 * 3 * A_WIDTH), row),
            pl.BlockSpec((tm, 2048), row),
            pl.BlockSpec((tm, C_CONV_CH), row),
            pl.BlockSpec((tm, C_WIDTH), row),
            pl.BlockSpec((tm, DT_COLS), row),
            pl.BlockSpec((tm // CHUNK, DT_COLS, CHUNK), lambda i: (i, 0, 0)),
        ],
        out_shape=[
            jax.ShapeDtypeStruct((T, 3 * A_WIDTH), BF16),
            jax.ShapeDtypeStruct((T // d4, d4 * 3 * A_WIDTH), BF16),
            jax.ShapeDtypeStruct((T // d16, d16 * 3 * A_WIDTH), BF16),
            jax.ShapeDtypeStruct((T, 2048), BF16),
            jax.ShapeDtypeStruct((T, C_CONV_CH), BF16),
            jax.ShapeDtypeStruct((T, C_WIDTH), BF16),
            jax.ShapeDtypeStruct((T, DT_COLS), F32),
            jax.ShapeDtypeStruct((T // CHUNK, DT_COLS, CHUNK), F32),
        ],
        scratch_shapes=[pltpu.VMEM((3 * A_WIDTH // LANES, tm, LANES), F32)],
        compiler_params=pltpu.CompilerParams(
            dimension_semantics=("arbitrary",), vmem_limit_bytes=VMEM_LIMIT_BYTES),
        name="in_proj",
    )(x2d, x2d, x2d, nw, cos_t, sin_t, w_main, w_dt, w_dtT, conv_w, conv_b)


def _attn_kernel(q_ref, kp_ref, kc_ref, kn_ref, vp_ref, vc_ref, vn_ref, o_ref, lse_ref,
                 kbuf, vbuf, *, qb, sub_len):
    n = pl.program_id(2)
    kbuf[0:ATT_Q, :] = kp_ref[...]
    kbuf[ATT_Q:ATT_Q + qb, :] = kc_ref[...]
    kbuf[ATT_Q + qb:, :] = kn_ref[...]
    vbuf[0:ATT_Q, :] = vp_ref[...]
    vbuf[ATT_Q:ATT_Q + qb, :] = vc_ref[...]
    vbuf[ATT_Q + qb:, :] = vn_ref[...]

    nkeys = ATT_Q + 2 * ATT_HALF
    qq = lax.broadcasted_iota(jnp.int32, (ATT_Q, nkeys), 0)
    kk = lax.broadcasted_iota(jnp.int32, (ATT_Q, nkeys), 1)
    kk1 = lax.broadcasted_iota(jnp.int32, (1, nkeys), 1)
    lane = lax.broadcasted_iota(jnp.int32, (ATT_Q, LANES), 1)
    low_half = lane < HEAD_DIM
    high_half = jnp.logical_not(low_half)
    band_bias = jnp.where((kk >= qq) & (kk <= qq + 2 * ATT_HALF), 0.0, NEG)

    for i in range(qb // ATT_Q):
        r0 = i * ATT_Q
        k0 = i * ATT_Q + ATT_Q - ATT_HALF
        q = q_ref[r0:r0 + ATT_Q, :]
        kw = kbuf[k0:k0 + nkeys, :]
        vw = vbuf[k0:k0 + nkeys, :]
        pos = kk1 + (n * qb + i * ATT_Q - ATT_HALF)
        bias = band_bias + jnp.where((pos >= 0) & (pos < sub_len), 0.0, NEG)
        stats = jnp.zeros((ATT_Q, LANES), F32)
        for j in range(A_WIDTH // LANES):
            cols = slice(j * LANES, (j + 1) * LANES)
            qp, kp, vp = q[:, cols], kw[:, cols], vw[:, cols]
            zero = jnp.zeros_like(qp)
            qs = jnp.concatenate([jnp.where(low_half, qp, zero), jnp.where(high_half, qp, zero)], axis=0)
            s = _dot_nt(qs, kp)
            ps = []
            for e in range(2):
                se = s[e * ATT_Q:(e + 1) * ATT_Q, :] + bias
                m = jnp.max(se, axis=-1, keepdims=True)
                p = jnp.exp2(se - m)
                den = jnp.sum(p, axis=-1, keepdims=True)
                ps.append(p.astype(BF16))
                stats = jnp.where(lane == 2 * j + e, m, stats)
                stats = jnp.where(lane == A_HEADS + 2 * j + e, den, stats)
            o = _dot(jnp.concatenate(ps, axis=0), vp)
            o_ref[r0:r0 + ATT_Q, cols] = jnp.where(low_half, o[0:ATT_Q, :], o[ATT_Q:, :]).astype(BF16)
        lse_ref[r0:r0 + ATT_Q, :] = stats


def _attention(view, dil):
    Bn, L, _ = view.shape
    qb = min(ATT_QB, L)
    nb = L // qb
    r128 = qb // ATT_Q
    last128 = L // ATT_Q - 1

    def cur(c):
        return pl.BlockSpec((None, qb, A_WIDTH), lambda b, r, n: (b, n, 3 * r + c))

    def prev(c):
        return pl.BlockSpec((None, ATT_Q, A_WIDTH),
                            lambda b, r, n: (b, jnp.maximum(n * r128 - 1, 0), 3 * r + c))

    def nxt(c):
        return pl.BlockSpec((None, ATT_Q, A_WIDTH),
                            lambda b, r, n: (b, jnp.minimum((n + 1) * r128, last128), 3 * r + c))

    return pl.pallas_call(
        functools.partial(_attn_kernel, qb=qb, sub_len=L),
        grid=(Bn, dil, nb),
        in_specs=[cur(0), prev(1), cur(1), nxt(1), prev(2), cur(2), nxt(2)],
        out_specs=[
            pl.BlockSpec((None, qb, A_WIDTH), lambda b, r, n: (b, n, r)),
            pl.BlockSpec((None, qb, LANES), lambda b, r, n: (b, n, r)),
        ],
        out_shape=[
            jax.ShapeDtypeStruct((Bn, L, dil * A_WIDTH), BF16),
            jax.ShapeDtypeStruct((Bn, L, dil * LANES), F32),
        ],
        scratch_shapes=[
            pltpu.VMEM((qb + 2 * ATT_Q, A_WIDTH), BF16),
            pltpu.VMEM((qb + 2 * ATT_Q, A_WIDTH), BF16),
        ],
        compiler_params=pltpu.CompilerParams(
            dimension_semantics=("arbitrary", "arbitrary", "arbitrary"),
            vmem_limit_bytes=VMEM_LIMIT_BYTES),
        name=f"attn_d{dil}",
    )(view, view, view, view, view, view, view)


def _ssd_chunk(xs, bm, cm, dt_raw, dt_rawT, bias_row, bias_col, alog_row, alog_col,
               state_ref, reverse):
    L = CHUNK
    dt = _softplus(dt_raw + bias_row)
    dtT = _softplus(dt_rawT + bias_col)
    dtA = dt * (-jnp.exp(alog_row))
    dtAT = dtT * (-jnp.exp(alog_col))
    ri = lax.broadcasted_iota(jnp.int32, (L, L), 0)
    ci = lax.broadcasted_iota(jnp.int32, (L, L), 1)
    tri = (ci >= ri) if reverse else (ci <= ri)
    tri_bf = jnp.where(tri, 1.0, 0.0).astype(BF16)
    acum = sum(_dot(tri_bf, t) for t in _split3(dtA))
    acumT = sum(_dot_nt(t, tri_bf) for t in _split3(dtAT))
    e = 0 if reverse else L - 1
    edge_row = acum[e:e + 1, :]
    edge_col = acumT[:, e:e + 1]
    wT = jnp.exp(edge_col - acumT) * dtT
    cdec = jnp.exp(edge_row)
    a2 = acum * LOG2E
    srcT = acumT * LOG2E - jnp.log2(dtT)
    wT_bf = wT.astype(BF16)
    lane = lax.broadcasted_iota(jnp.int32, (L, LANES), 1)
    low_half = lane < HEAD_DIM
    high_half = jnp.logical_not(low_half)

    def block_diag(t):
        zero = jnp.zeros_like(t)
        return jnp.concatenate([jnp.where(low_half, t, zero), jnp.where(high_half, t, zero)], axis=0)

    ys = []
    for g in range(C_GROUPS):
        bg = bm[:, g * C_STATE:(g + 1) * C_STATE]
        cg = cm[:, g * C_STATE:(g + 1) * C_STATE]
        cb = _dot_nt(cg, bg).astype(BF16)
        bgT = bg.astype(F32).T.astype(BF16)
        for pr in range(4):
            pair = g * 4 + pr
            cols = slice(pair * LANES, (pair + 1) * LANES)
            xbd = block_diag(xs[:, cols])
            prev = state_ref[:, cols]
            pbd = block_diag(prev.astype(BF16))
            ws, css, bss, cd = [], [], [], []
            for k in range(2):
                h = 2 * pair + k
                col = jnp.broadcast_to(a2[:, h:h + 1], (L, L))
                dec = jnp.exp2(jnp.where(tri, col - srcT[h:h + 1, :], NEG))
                ws.append(cb * dec.astype(BF16))
                css.append(cg * jnp.exp2(col).astype(BF16))
                bss.append(bgT * jnp.broadcast_to(wT_bf[h:h + 1, :], (C_STATE, L)))
                cd.append(cdec[:, h:h + 1])
            ys.append(_dot(jnp.concatenate(ws + css, axis=1), jnp.concatenate([xbd, pbd], axis=0)))
            decay = jnp.where(low_half[0:1, :], cd[0], cd[1])
            state_ref[:, cols] = prev * decay + _dot(jnp.concatenate(bss, axis=1), xbd)
    return ys


def _ssd_fwd_kernel(xact_ref, dt_ref, dtT_ref, brow_ref, bcol_ref, arow_ref, acol_ref, dskip_ref,
                    yf_ref, state_ref, *, ts):
    s = pl.program_id(1)

    @pl.when(s == 0)
    def _():
        state_ref[...] = jnp.zeros_like(state_ref)

    brow, bcol = brow_ref[:, 0:C_HEADS], bcol_ref[0:C_HEADS, :]
    arow, acol = arow_ref[:, 0:C_HEADS], acol_ref[0:C_HEADS, :]

    def body(c, carry):
        r0 = pl.multiple_of(c * CHUNK, CHUNK)
        rows = pl.ds(r0, CHUNK)
        xs = xact_ref[rows, 0:C_WIDTH]
        bm = xact_ref[rows, C_WIDTH:C_WIDTH + C_GROUPS * C_STATE]
        cm = xact_ref[rows, C_WIDTH + C_GROUPS * C_STATE:C_CONV_CH]
        ys = _ssd_chunk(xs, bm, cm, dt_ref[rows, 0:C_HEADS], dtT_ref[c, 0:C_HEADS, :],
                        brow, bcol, arow, acol, state_ref, reverse=False)
        for pair, y in enumerate(ys):
            cols = slice(pair * LANES, (pair + 1) * LANES)
            y = y + dskip_ref[:, cols] * xs[:, cols].astype(F32)
            yf_ref[rows, cols] = y.astype(BF16)
        return carry

    lax.fori_loop(0, ts // CHUNK, body, 0)


def _ssd_fwd(xact3, dt3, dtT, brow, bcol, arow, acol, dskip):
    Bn, S, _ = xact3.shape
    ts = TS_SEQ
    ns = S // ts
    const = lambda b, s: (0, 0)
    tile = lambda b, s: (b, s, 0)
    return pl.pallas_call(
        functools.partial(_ssd_fwd_kernel, ts=ts),
        grid=(Bn, ns),
        in_specs=[
            pl.BlockSpec((None, ts, C_CONV_CH), tile),
            pl.BlockSpec((None, ts, DT_COLS), tile),
            pl.BlockSpec((ts // CHUNK, DT_COLS, CHUNK), lambda b, s: (b * ns + s, 0, 0)),
            pl.BlockSpec((1, DT_COLS), const),
            pl.BlockSpec((DT_COLS, 1), const),
            pl.BlockSpec((1, DT_COLS), const),
            pl.BlockSpec((DT_COLS, 1), const),
            pl.BlockSpec((1, C_WIDTH), const),
        ],
        out_specs=pl.BlockSpec((None, ts, C_WIDTH), tile),
        out_shape=jax.ShapeDtypeStruct((Bn, S, C_WIDTH), BF16),
        scratch_shapes=[pltpu.VMEM((C_STATE, C_WIDTH), F32)],
        compiler_params=pltpu.CompilerParams(
            dimension_semantics=("arbitrary", "arbitrary"), vmem_limit_bytes=VMEM_LIMIT_BYTES),
        name="ssd_fwd",
    )(xact3, dt3, dtT, brow, bcol, arow, acol, dskip)


def _mix_out_kernel(xact_ref, dt_ref, dtT_ref, yf_ref, zc_ref, zab_ref,
                    o1_ref, o4_ref, o16_ref, l1_ref, l4_ref, l16_ref, h_ref,
                    brow_ref, bcol_ref, arow_ref, acol_ref,
                    sguw_ref, sgub_ref, e8_ref, nrm_ref, wout_ref, fnw_ref,
                    out_ref, state_ref, ybuf_ref, mixed_ref, onat_ref, snat_ref, *, ts, final):
    s = pl.program_id(1)

    @pl.when(s == 0)
    def _():
        state_ref[...] = jnp.zeros_like(state_ref)

    slabs = A_WIDTH // LANES
    for ci, (d, o_ref, l_ref) in enumerate(((ATT_CFGS[1][1], o4_ref, l4_ref),
                                            (ATT_CFGS[2][1], o16_ref, l16_ref))):
        for r in range(d):
            dst_rows = pl.ds(r, ts // d, stride=d)
            for j in range(slabs):
                c0 = r * A_WIDTH + j * LANES
                onat_ref[ci, j, dst_rows, :] = o_ref[:, c0:c0 + LANES].astype(F32)
            snat_ref[ci, dst_rows, :] = l_ref[:, r * LANES:(r + 1) * LANES]

    brow, bcol = brow_ref[:, C_HEADS:DT_COLS], bcol_ref[C_HEADS:DT_COLS, :]
    arow, acol = arow_ref[:, C_HEADS:DT_COLS], acol_ref[C_HEADS:DT_COLS, :]
    nch = ts // CHUNK

    def body(i, carry):
        c = nch - 1 - i
        r0 = pl.multiple_of(c * CHUNK, CHUNK)
        rows = pl.ds(r0, CHUNK)

        xs = xact_ref[rows, 0:C_WIDTH]
        bm = xact_ref[rows, C_WIDTH:C_WIDTH + C_GROUPS * C_STATE]
        cm = xact_ref[rows, C_WIDTH + C_GROUPS * C_STATE:C_CONV_CH]
        ys = _ssd_chunk(xs, bm, cm, dt_ref[rows, C_HEADS:DT_COLS], dtT_ref[c, C_HEADS:DT_COLS, :],
                        brow, bcol, arow, acol, state_ref, reverse=True)
        ssq = jnp.zeros((CHUNK, 1), F32)
        for pair, y in enumerate(ys):
            cols = slice(pair * LANES, (pair + 1) * LANES)
            y = (y + yf_ref[rows, cols].astype(F32)) * _silu(zc_ref[rows, cols].astype(F32))
            ybuf_ref[:, cols] = y
            ssq = ssq + jnp.sum(y * y, axis=-1, keepdims=True)
        inv = lax.rsqrt(ssq * (1.0 / C_WIDTH) + EPS)
        mixed_ref[rows, A_WIDTH + B_WIDTH:D_MIX] = (ybuf_ref[...] * inv * nrm_ref[...]).astype(BF16)

        st = [l1_ref[rows, :], snat_ref[0, rows, :], snat_ref[1, rows, :]]
        mx = jnp.maximum(jnp.maximum(st[0], st[1]), st[2])
        es = [jnp.exp2(t - mx) for t in st]
        dens = [pltpu.roll(t, LANES - A_HEADS, 1) for t in st]
        z = es[0] * dens[0] + es[1] * dens[1] + es[2] * dens[2]
        head_lane = lax.broadcasted_iota(jnp.int32, (CHUNK, LANES), 1) < A_HEADS
        rz = 1.0 / jnp.where(head_lane, z, 1.0)
        wexp = [_dot((ec * rz).astype(BF16), e8_ref[...]) for ec in es]
        for j in range(A_WIDTH // LANES):
            cols = slice(j * LANES, (j + 1) * LANES)
            oa = (wexp[0][:, cols] * o1_ref[rows, cols].astype(F32)
                  + wexp[1][:, cols] * onat_ref[0, j, rows, :]
                  + wexp[2][:, cols] * onat_ref[1, j, rows, :])
            za = zab_ref[rows, cols].astype(F32)
            mixed_ref[rows, cols] = (oa * _silu(za)).astype(BF16)

        for g in range(B_GROUPS):
            cols = slice(g * LANES, (g + 1) * LANES)
            ub = zab_ref[rows, A_WIDTH + g * LANES:A_WIDTH + (g + 1) * LANES].astype(F32)
            vb = zab_ref[rows, 2 * A_WIDTH + g * LANES:2 * A_WIDTH + (g + 1) * LANES]
            zb = zab_ref[rows, 3 * A_WIDTH + g * LANES:3 * A_WIDTH + (g + 1) * LANES].astype(F32)
            mixed = _dot(sguw_ref[g], vb) + sgub_ref[:, cols]
            mixed_ref[rows, A_WIDTH + g * LANES:A_WIDTH + (g + 1) * LANES] = (
                ub * mixed * _silu(zb)).astype(BF16)
        return carry

    lax.fori_loop(0, nch, body, 0)

    hn = h_ref[...] + _dot(mixed_ref[...], wout_ref[...])
    if final:
        ms = jnp.mean(hn * hn, axis=-1, keepdims=True)
        hn = hn * lax.rsqrt(ms + EPS) * fnw_ref[...]
    out_ref[...] = hn


def _mix_out(xact3, dt3, dtT, yf3, zc3, zab3, o1, o4, o16, l1, l4, l16, h3,
             brow, bcol, arow, acol, sguw, sgub, e8, nrm, wout, fnw, final):
    Bn, S, _ = xact3.shape
    ts = TS_SEQ
    ns = S // ts
    const2 = lambda b, s: (0, 0)
    const3 = lambda b, s: (0, 0, 0)
    tile = lambda b, s: (b, ns - 1 - s, 0)
    tspec = lambda w: pl.BlockSpec((None, ts, w), tile)
    gspec = lambda d, w: pl.BlockSpec((None, ts // d, d * w), tile)
    d4, d16 = ATT_CFGS[1][1], ATT_CFGS[2][1]
    return pl.pallas_call(
        functools.partial(_mix_out_kernel, ts=ts, final=final),
        grid=(Bn, ns),
        in_specs=[
            tspec(C_CONV_CH), tspec(DT_COLS),
            pl.BlockSpec((ts // CHUNK, DT_COLS, CHUNK), lambda b, s: (b * ns + ns - 1 - s, 0, 0)),
            tspec(C_WIDTH), tspec(C_WIDTH), tspec(2048),
            tspec(A_WIDTH), gspec(d4, A_WIDTH), gspec(d16, A_WIDTH),
            tspec(LANES), gspec(d4, LANES), gspec(d16, LANES),
            tspec(D_MODEL),
            pl.BlockSpec((1, DT_COLS), const2),
            pl.BlockSpec((DT_COLS, 1), const2),
            pl.BlockSpec((1, DT_COLS), const2),
            pl.BlockSpec((DT_COLS, 1), const2),
            pl.BlockSpec((B_GROUPS, CHUNK, CHUNK), const3),
            pl.BlockSpec((CHUNK, B_WIDTH), const2),
            pl.BlockSpec((LANES, A_WIDTH), const2),
            pl.BlockSpec((1, C_WIDTH), const2),
            pl.BlockSpec((D_MIX, D_MODEL), const2),
            pl.BlockSpec((1, D_MODEL), const2),
        ],
        out_specs=pl.BlockSpec((None, ts, D_MODEL), tile),
        out_shape=jax.ShapeDtypeStruct((Bn, S, D_MODEL), F32),
        scratch_shapes=[
            pltpu.VMEM((C_STATE, C_WIDTH), F32),
            pltpu.VMEM((CHUNK, C_WIDTH), F32),
            pltpu.VMEM((ts, D_MIX), BF16),
            pltpu.VMEM((2, A_WIDTH // LANES, ts, LANES), F32),
            pltpu.VMEM((2, ts, LANES), F32),
        ],
        compiler_params=pltpu.CompilerParams(
            dimension_semantics=("arbitrary", "arbitrary"), vmem_limit_bytes=VMEM_LIMIT_BYTES),
        name="mix_out_final" if final else "mix_out",
    )(xact3, dt3, dtT, yf3, zc3, zab3, o1, o4, o16, l1, l4, l16, h3,
      brow, bcol, arow, acol, sguw, sgub, e8, nrm, wout, fnw)


def _rope_tables(S):
    inv = ROPE_THETA ** (-jnp.arange(0, HEAD_DIM, 2, dtype=F32) / HEAD_DIM)
    ang = jnp.arange(S, dtype=F32)[:, None] * inv[None, :]
    cos, sin = jnp.cos(ang), jnp.sin(ang)
    cos_t = jnp.concatenate([cos, cos, cos, cos], axis=-1)
    sin_t = jnp.concatenate([-sin, sin, -sin, sin], axis=-1)
    return cos_t, sin_t


def _layer_params(l, norm_w, w_in, sgu_w, sgu_b, conv_w, conv_b, dt_bias, a_log, d_skip,
                  ssd_norm_w, w_out):
    w = w_in[l]
    head_of_lane = jnp.arange(A_WIDTH) // HEAD_DIM
    e8 = (jnp.arange(LANES)[:, None] == head_of_lane[None, :]).astype(BF16)
    return dict(
        nw=norm_w[l][None, :],
        w_main=w[:, :MAIN_COLS].astype(BF16),
        w_dt=w[:, MAIN_COLS:].astype(BF16),
        w_dtT=w[:, MAIN_COLS:].T.astype(BF16),
        sguw=sgu_w[l].astype(BF16),
        sgub=jnp.repeat(sgu_b[l].T, B_WIDTH // B_GROUPS, axis=1),
        conv_w=conv_w[l],
        conv_b=conv_b[l][None, :],
        brow=dt_bias[l].reshape(1, DT_COLS),
        bcol=dt_bias[l].reshape(DT_COLS, 1),
        arow=a_log[l].reshape(1, DT_COLS),
        acol=a_log[l].reshape(DT_COLS, 1),
        dskip=jnp.repeat(d_skip[l], HEAD_DIM)[None, :],
        nrm=ssd_norm_w[l][None, :],
        wout=w_out[l].astype(BF16),
        e8=e8,
    )


def _trunk(x, layers, fnw):
    Bn, S, _ = x.shape
    T = Bn * S
    assert S % (ATT_Q * max(d for _, d in ATT_CFGS)) == 0 and S % TS_SEQ == 0 and S % TM_PROJ == 0
    cos_t, sin_t = _rope_tables(S)
    h = x
    for li, p in enumerate(layers):
        qkv, qkv4, qkv16, zab, xact, zc, dt, dtT = _in_proj(
            h.reshape(T, D_MODEL), S, p["nw"], cos_t, sin_t, p["w_main"], p["w_dt"], p["w_dtT"],
            p["conv_w"], p["conv_b"])
        att = [_attention(t.reshape(Bn, S // dil, dil * 3 * A_WIDTH), dil)
               for t, (_, dil) in zip((qkv, qkv4, qkv16), ATT_CFGS)]
        xact3 = xact.reshape(Bn, S, C_CONV_CH)
        dt3 = dt.reshape(Bn, S, DT_COLS)
        yf3 = _ssd_fwd(xact3, dt3, dtT, p["brow"], p["bcol"], p["arow"], p["acol"], p["dskip"])
        h = _mix_out(xact3, dt3, dtT, yf3, zc.reshape(Bn, S, C_WIDTH), zab.reshape(Bn, S, 2048),
                     att[0][0], att[1][0], att[2][0], att[0][1], att[1][1], att[2][1], h,
                     p["brow"], p["bcol"], p["arow"], p["acol"], p["sguw"], p["sgub"], p["e8"],
                     p["nrm"], p["wout"], fnw, final=(li == len(layers) - 1))
    return h


def kernel(x_prompt, x_sample, norm_w, w_in, sgu_w, sgu_b, conv_w, conv_b, dt_bias, a_log, d_skip,
           ssd_norm_w, w_out, final_norm_w):
    depth = w_in.shape[0]
    layers = [_layer_params(l, norm_w, w_in, sgu_w, sgu_b, conv_w, conv_b, dt_bias, a_log, d_skip,
                            ssd_norm_w, w_out) for l in range(depth)]
    fnw = final_norm_w[None, :]
    return (_trunk(x_prompt, layers, fnw), _trunk(x_sample, layers, fnw))
```

```python
import functools

import jax
import jax.numpy as jnp
from jax import lax
from jax.experimental import pallas as pl
from jax.experimental.pallas import tpu as pltpu

F32 = jnp.float32
BF16 = jnp.bfloat16

D_MODEL = 1024
HEAD_DIM = 64
A_WIDTH = 512
A_HEADS = 8
ATT_CFGS = ((128, 1), (512, 4), (2048, 16))
ATT_Q = 128
ATT_HALF = 64
ROPE_THETA = 10000.0
B_WIDTH = 512
B_GROUPS = 4
CHUNK = 128
C_WIDTH = 1024
C_HEADS = 16
C_STATE = 128
C_GROUPS = 2
C_CONV_CH = C_WIDTH + 2 * C_GROUPS * C_STATE
MAIN_COLS = 6144
DT_COLS = 2 * C_HEADS
D_MIX = 2048
EPS = 1e-5
NEG = -1e30
LOG2E = 1.4426950408889634
LN2 = 0.6931471805599453

VMEM_LIMIT_BYTES = 56 * 1024 * 1024
LANES = 128
F32_SUBLANES = 8
EDGE_ROWS = 16
MXU_COLS = 256

TM_PROJ = 512
TS_SEQ = 512
ATT_QB = 512
ATT_SUBQ = 64


def _silu(x):
    hx = 0.5 * x
    return hx + hx * jnp.tanh(hx)


def _softplus(x):
    return jnp.maximum(x, 0.0) + jnp.log1p(jnp.exp(-jnp.abs(x)))


def _split3(x):
    hi = x.astype(BF16)
    r1 = x - hi.astype(F32)
    mid = r1.astype(BF16)
    lo = (r1 - mid.astype(F32)).astype(BF16)
    return hi, mid, lo


def _dot(a, b):
    return jnp.dot(a, b, preferred_element_type=F32)


def _dot_nt(a, b):
    return lax.dot_general(a, b, (((1,), (1,)), ((), ())), preferred_element_type=F32)


def _inproj_kernel(x_ref, xprev_ref, xnext_ref, nw_ref, cos_ref, sin_ref, w_ref, wdt_ref, wdtT_ref,
                   cw_ref, cb_ref,
                   qkv_ref, qkv4_ref, qkv16_ref, zab_ref, xact_ref, zc_ref, dt_ref, dtT_ref,
                   qkv32_ref, by4_ref, *, tm, tiles_per_seq):
    ti = pl.program_id(0) % tiles_per_seq
    d4, d16 = ATT_CFGS[1][1], ATT_CFGS[2][1]
    x = jnp.concatenate([x_ref[...], xprev_ref[...], xnext_ref[...]], axis=0)
    ms = jnp.mean(x * x, axis=-1, keepdims=True)
    xn_ext = (x * lax.rsqrt(ms + EPS) * nw_ref[...]).astype(BF16)
    xn = xn_ext[0:tm, :]
    cos = cos_ref[...]
    sin = sin_ref[...]
    lane = lax.broadcasted_iota(jnp.int32, cos.shape, 1)
    first_half = (lane % HEAD_DIM) < (HEAD_DIM // 2)
    gw = MXU_COLS
    gslabs = gw // LANES
    qk_cols, qkv_cols = 2 * A_WIDTH, 3 * A_WIDTH
    zab_end, xbc_end = qkv_cols + 2048, qkv_cols + 2048 + C_CONV_CH
    erow = lax.broadcasted_iota(jnp.int32, (EDGE_ROWS, gw), 0)

    groups = list(range(0, MAIN_COLS, gw))
    heavy = [c for c in groups if c < qkv_cols or zab_end <= c < xbc_end]
    plain = [c for c in groups if c not in heavy]
    assert len(heavy) == len(plain)
    for c0 in [c for pair in zip(heavy, plain) for c in pair]:
        if c0 < qkv_cols:
            acc = _dot(xn, w_ref[:, c0:c0 + gw])
            scale = HEAD_DIM ** -0.5 * LOG2E if c0 < A_WIDTH else 1.0
            for j in range(gslabs):
                slab = c0 // LANES + j
                out = acc[:, j * LANES:(j + 1) * LANES]
                if c0 < qk_cols:
                    fwd = pltpu.roll(out, HEAD_DIM // 2, 1)
                    bwd = pltpu.roll(out, LANES - HEAD_DIM // 2, 1)
                    out = out * cos + jnp.where(first_half, bwd, fwd) * sin
                    if scale != 1.0:
                        out = out * scale
                qkv32_ref[slab] = out
                qkv_ref[:, slab * LANES:(slab + 1) * LANES] = out.astype(BF16)
                n4 = tm // d4
                for r in range(d4):
                    piece = qkv32_ref[slab, pl.ds(r, n4, stride=d4), :]
                    by4_ref[slab, r * n4:(r + 1) * n4, :] = piece
                    c = r * qkv_cols + slab * LANES
                    qkv4_ref[:, c:c + LANES] = piece.astype(BF16)
                for r in range(d16):
                    piece = by4_ref[slab, pl.ds((r % d4) * n4 + r // d4, tm // d16, stride=d16 // d4), :]
                    c = r * qkv_cols + slab * LANES
                    qkv16_ref[:, c:c + LANES] = piece.astype(BF16)
        elif c0 < zab_end:
            zab_ref[:, c0 - qkv_cols:c0 - qkv_cols + gw] = _dot(xn, w_ref[:, c0:c0 + gw]).astype(BF16)
        elif c0 < xbc_end:
            cols = slice(c0 - zab_end, c0 - zab_end + gw)
            ext = _dot(xn_ext, w_ref[:, c0:c0 + gw])
            cur = ext[0:tm, :]
            prev_row = jnp.where(ti > 0, ext[tm + 7:tm + 8, :], 0.0)
            next_row = jnp.where(ti < tiles_per_seq - 1, ext[tm + 8:tm + 9, :], 0.0)

            def conv_act(xm1, x0, xp1):
                conv = (cw_ref[0:1, cols] * xm1 + cw_ref[1:2, cols] * x0 + cw_ref[2:3, cols] * xp1
                        + cb_ref[:, cols])
                return _silu(conv).astype(BF16)

            xact_ref[:, cols] = conv_act(pltpu.roll(cur, 1, 0), cur, pltpu.roll(cur, tm - 1, 0))
            e = EDGE_ROWS
            top, bot = cur[0:e, :], cur[tm - e:tm, :]
            first, last = erow == 0, erow == e - 1
            top_m1 = jnp.where(first, prev_row, pltpu.roll(top, 1, 0))
            top_p1 = jnp.where(last, cur[e:e + 1, :], pltpu.roll(top, e - 1, 0))
            xact_ref[0:e, cols] = conv_act(top_m1, top, top_p1)
            bot_m1 = jnp.where(first, cur[tm - e - 1:tm - e, :], pltpu.roll(bot, 1, 0))
            bot_p1 = jnp.where(last, next_row, pltpu.roll(bot, e - 1, 0))
            xact_ref[tm - e:tm, cols] = conv_act(bot_m1, bot, bot_p1)
        else:
            zc_ref[:, c0 - xbc_end:c0 - xbc_end + gw] = _dot(xn, w_ref[:, c0:c0 + gw]).astype(BF16)
    dt_ref[...] = _dot(xn, wdt_ref[...])
    dtT = _dot_nt(wdtT_ref[...], xn)
    for c in range(dtT_ref.shape[0]):
        dtT_ref[c] = dtT[:, c * CHUNK:(c + 1) * CHUNK]


def _in_proj(x2d, seq_len, nw, cos_t, sin_t, w_main, w_dt, w_dtT, conv_w, conv_b):
    T = x2d.shape[0]
    tm = TM_PROJ
    tiles_per_seq = seq_len // tm
    hb = tm // F32_SUBLANES
    last_hb = T // F32_SUBLANES - 1
    d4, d16 = ATT_CFGS[1][1], ATT_CFGS[2][1]
    const = lambda i: (0, 0)
    row = lambda i: (i, 0)
    return pl.pallas_call(
        functools.partial(_inproj_kernel, tm=tm, tiles_per_seq=tiles_per_seq),
        grid=(T // tm,),
        in_specs=[
            pl.BlockSpec((tm, D_MODEL), row),
            pl.BlockSpec((F32_SUBLANES, D_MODEL), lambda i: (jnp.maximum(i * hb - 1, 0), 0)),
            pl.BlockSpec((F32_SUBLANES, D_MODEL), lambda i: (jnp.minimum((i + 1) * hb, last_hb), 0)),
            pl.BlockSpec((1, D_MODEL), const),
            pl.BlockSpec((tm, LANES), lambda i: (i % tiles_per_seq, 0)),
            pl.BlockSpec((tm, LANES), lambda i: (i % tiles_per_seq, 0)),
            pl.BlockSpec((D_MODEL, MAIN_COLS), const),
            pl.BlockSpec((D_MODEL, DT_COLS), const),
            pl.BlockSpec((DT_COLS, D_MODEL), const),
            pl.BlockSpec((3, C_CONV_CH), const),
            pl.BlockSpec((1, C_CONV_CH), const),
        ],
        out_specs=[
            pl.BlockSpec((tm, 3 * A_WIDTH), row),
            pl.BlockSpec((tm // d4, d4 * 3 * A_WIDTH), row),
            pl.BlockSpec((tm // d16, d16 * 3 * A_WIDTH), row),
            pl.BlockSpec((tm, 2048), row),
            pl.BlockSpec((tm, C_CONV_CH), row),
            pl.BlockSpec((tm, C_WIDTH), row),
            pl.BlockSpec((tm, DT_COLS), row),
            pl.BlockSpec((tm // CHUNK, DT_COLS, CHUNK), lambda i: (i, 0, 0)),
        ],
        out_shape=[
            jax.ShapeDtypeStruct((T, 3 * A_WIDTH), BF16),
            jax.ShapeDtypeStruct((T // d4, d4 * 3 * A_WIDTH), BF16),
            jax.ShapeDtypeStruct((T // d16, d16 * 3 * A_WIDTH), BF16),
            jax.ShapeDtypeStruct((T, 2048), BF16),
            jax.ShapeDtypeStruct((T, C_CONV_CH), BF16),
            jax.ShapeDtypeStruct((T, C_WIDTH), BF16),
            jax.ShapeDtypeStruct((T, DT_COLS), F32),
            jax.ShapeDtypeStruct((T // CHUNK, DT_COLS, CHUNK), F32),
        ],
        scratch_shapes=[pltpu.VMEM((3 * A_WIDTH // LANES, tm, LANES), F32),
                        pltpu.VMEM((3 * A_WIDTH // LANES, tm, LANES), F32)],
        compiler_params=pltpu.CompilerParams(
            dimension_semantics=("arbitrary",), vmem_limit_bytes=VMEM_LIMIT_BYTES),
        name="in_proj",
    )(x2d, x2d, x2d, nw, cos_t, sin_t, w_main, w_dt, w_dtT, conv_w, conv_b)


def _attn_kernel(q_ref, kp_ref, kc_ref, kn_ref, vp_ref, vc_ref, vn_ref, o_ref, lse_ref,
                 kbuf, vbuf, *, qb, sub_len):
    n = pl.program_id(2)
    kbuf[0:ATT_Q, :] = kp_ref[...]
    kbuf[ATT_Q:ATT_Q + qb, :] = kc_ref[...]
    kbuf[ATT_Q + qb:, :] = kn_ref[...]
    vbuf[0:ATT_Q, :] = vp_ref[...]
    vbuf[ATT_Q:ATT_Q + qb, :] = vc_ref[...]
    vbuf[ATT_Q + qb:, :] = vn_ref[...]

    sq = ATT_SUBQ
    nkeys = MXU_COLS
    assert sq + 2 * ATT_HALF <= nkeys
    qq = lax.broadcasted_iota(jnp.int32, (sq, nkeys), 0)
    kk = lax.broadcasted_iota(jnp.int32, (sq, nkeys), 1)
    kk1 = lax.broadcasted_iota(jnp.int32, (1, nkeys), 1)
    lane = lax.broadcasted_iota(jnp.int32, (sq, LANES), 1)
    low_half = lane < HEAD_DIM
    high_half = jnp.logical_not(low_half)
    band_bias = jnp.where((kk >= qq) & (kk <= qq + 2 * ATT_HALF), 0.0, NEG)
    ones = jnp.ones((nkeys, LANES), BF16)

    for i in range(qb // sq):
        r0 = i * sq
        k0 = i * sq + ATT_Q - ATT_HALF
        q = q_ref[r0:r0 + sq, :]
        kw = kbuf[k0:k0 + nkeys, :]
        vw = vbuf[k0:k0 + nkeys, :]
        pos = kk1 + (n * qb + i * sq - ATT_HALF)
        bias = band_bias + jnp.where((pos >= 0) & (pos < sub_len), 0.0, NEG)
        rows = slice(r0, r0 + sq)
        lse_ref[rows, :] = jnp.zeros((sq, LANES), F32)
        for j in range(A_WIDTH // LANES):
            cols = slice(j * LANES, (j + 1) * LANES)
            qp, kp, vp = q[:, cols], kw[:, cols], vw[:, cols]
            zero = jnp.zeros_like(qp)
            qs = jnp.concatenate([jnp.where(low_half, qp, zero), jnp.where(high_half, qp, zero)], axis=0)
            s = _dot_nt(qs, kp)
            ps = []
            for e in range(2):
                se = s[e * sq:(e + 1) * sq, :] + bias
                m = jnp.max(se, axis=-1, keepdims=True)
                ps.append(jnp.exp2(se - m).astype(BF16))
                lse_ref[rows, 2 * j + e:2 * j + e + 1] = m
            o = _dot(jnp.concatenate(ps, axis=0), jnp.concatenate([vp, ones], axis=1))
            for e in range(2):
                h = A_HEADS + 2 * j + e
                lse_ref[rows, h:h + 1] = o[e * sq:(e + 1) * sq, LANES + h:LANES + h + 1]
            o_ref[rows, cols] = jnp.where(low_half, o[0:sq, 0:LANES], o[sq:, 0:LANES]).astype(BF16)


def _attention(view, dil):
    Bn, L, _ = view.shape
    qb = min(ATT_QB, L)
    nb = L // qb
    r128 = qb // ATT_Q
    last128 = L // ATT_Q - 1

    def cur(c):
        return pl.BlockSpec((None, qb, A_WIDTH), lambda b, r, n: (b, n, 3 * r + c))

    def prev(c):
        return pl.BlockSpec((None, ATT_Q, A_WIDTH),
                            lambda b, r, n: (b, jnp.maximum(n * r128 - 1, 0), 3 * r + c))

    def nxt(c):
        return pl.BlockSpec((None, ATT_Q, A_WIDTH),
                            lambda b, r, n: (b, jnp.minimum((n + 1) * r128, last128), 3 * r + c))

    return pl.pallas_call(
        functools.partial(_attn_kernel, qb=qb, sub_len=L),
        grid=(Bn, dil, nb),
        in_specs=[cur(0), prev(1), cur(1), nxt(1), prev(2), cur(2), nxt(2)],
        out_specs=[
            pl.BlockSpec((None, qb, A_WIDTH), lambda b, r, n: (b, n, r)),
            pl.BlockSpec((None, qb, LANES), lambda b, r, n: (b, n, r)),
        ],
        out_shape=[
            jax.ShapeDtypeStruct((Bn, L, dil * A_WIDTH), BF16),
            jax.ShapeDtypeStruct((Bn, L, dil * LANES), F32),
        ],
        scratch_shapes=[
            pltpu.VMEM((qb + 2 * ATT_Q, A_WIDTH), BF16),
            pltpu.VMEM((qb + 2 * ATT_Q, A_WIDTH), BF16),
        ],
        compiler_params=pltpu.CompilerParams(
            dimension_semantics=("arbitrary", "arbitrary", "arbitrary"),
            vmem_limit_bytes=VMEM_LIMIT_BYTES),
        name=f"attn_d{dil}",
    )(view, view, view, view, view, view, view)


def _ssd_chunk(xs, bm, cm, dt_raw, dt_rawT, bias_row, bias_col, alog_row, alog_col,
               state_ref, reverse):
    L = CHUNK
    dt = _softplus(dt_raw + bias_row)
    dtT = _softplus(dt_rawT + bias_col)
    dtA = dt * (-jnp.exp(alog_row))
    dtAT = dtT * (-jnp.exp(alog_col))
    ri = lax.broadcasted_iota(jnp.int32, (L, L), 0)
    ci = lax.broadcasted_iota(jnp.int32, (L, L), 1)
    tri = (ci >= ri) if reverse else (ci <= ri)
    tri_bf = jnp.where(tri, 1.0, 0.0).astype(BF16)
    acum = sum(_dot(tri_bf, t) for t in _split3(dtA))
    acumT = sum(_dot_nt(t, tri_bf) for t in _split3(dtAT))
    e = 0 if reverse else L - 1
    edge_row = acum[e:e + 1, :]
    edge_col = acumT[:, e:e + 1]
    wT = jnp.exp(edge_col - acumT) * dtT
    cdec = jnp.exp(edge_row)
    a2 = acum * LOG2E
    srcT = acumT * LOG2E - jnp.log2(dtT)
    lane = lax.broadcasted_iota(jnp.int32, (L, LANES), 1)
    low_half = lane < HEAD_DIM
    high_half = jnp.logical_not(low_half)

    def block_diag(t):
        zero = jnp.zeros_like(t)
        return jnp.concatenate([jnp.where(low_half, t, zero), jnp.where(high_half, t, zero)], axis=0)

    ys = []
    for g in range(C_GROUPS):
        bg = bm[:, g * C_STATE:(g + 1) * C_STATE]
        cg = cm[:, g * C_STATE:(g + 1) * C_STATE]
        cb = _dot_nt(cg, bg).astype(BF16)
        bgT = bg.astype(F32).T.astype(BF16)
        for pr in range(4):
            pair = g * 4 + pr
            cols = slice(pair * LANES, (pair + 1) * LANES)
            xbd = block_diag(xs[:, cols])
            prev = state_ref[:, cols]
            pbd = block_diag(prev.astype(BF16))
            ws, css, bss, cd = [], [], [], []
            for k in range(2):
                h = 2 * pair + k
                col = jnp.broadcast_to(a2[:, h:h + 1], (L, L))
                dec = jnp.exp2(jnp.where(tri, col - srcT[h:h + 1, :], NEG))
                ws.append(cb * dec.astype(BF16))
                css.append(cg * jnp.exp2(col).astype(BF16))
                bss.append(bgT * jnp.broadcast_to(wT[h:h + 1, :], (C_STATE, L)).astype(BF16))
                cd.append(cdec[:, h:h + 1])
            ys.append(_dot(jnp.concatenate(ws + css, axis=1), jnp.concatenate([xbd, pbd], axis=0)))
            decay = jnp.where(low_half[0:1, :], cd[0], cd[1])
            state_ref[:, cols] = prev * decay + _dot(jnp.concatenate(bss, axis=1), xbd)
    return ys


def _ssd_fwd_kernel(xact_ref, dt_ref, dtT_ref, brow_ref, bcol_ref, arow_ref, acol_ref, dskip_ref,
                    yf_ref, state_ref, *, ts):
    s = pl.program_id(1)

    @pl.when(s == 0)
    def _():
        state_ref[...] = jnp.zeros_like(state_ref)

    brow, bcol = brow_ref[:, 0:C_HEADS], bcol_ref[0:C_HEADS, :]
    arow, acol = arow_ref[:, 0:C_HEADS], acol_ref[0:C_HEADS, :]

    def body(c, carry):
        r0 = pl.multiple_of(c * CHUNK, CHUNK)
        rows = pl.ds(r0, CHUNK)
        xs = xact_ref[rows, 0:C_WIDTH]
        bm = xact_ref[rows, C_WIDTH:C_WIDTH + C_GROUPS * C_STATE]
        cm = xact_ref[rows, C_WIDTH + C_GROUPS * C_STATE:C_CONV_CH]
        ys = _ssd_chunk(xs, bm, cm, dt_ref[rows, 0:C_HEADS], dtT_ref[c, 0:C_HEADS, :],
                        brow, bcol, arow, acol, state_ref, reverse=False)
        for pair, y in enumerate(ys):
            cols = slice(pair * LANES, (pair + 1) * LANES)
            y = y + dskip_ref[:, cols] * xs[:, cols].astype(F32)
            yf_ref[rows, cols] = y.astype(BF16)
        return carry

    lax.fori_loop(0, ts // CHUNK, body, 0)


def _ssd_fwd(xact3, dt3, dtT, brow, bcol, arow, acol, dskip):
    Bn, S, _ = xact3.shape
    ts = TS_SEQ
    ns = S // ts
    const = lambda b, s: (0, 0)
    tile = lambda b, s: (b, s, 0)
    return pl.pallas_call(
        functools.partial(_ssd_fwd_kernel, ts=ts),
        grid=(Bn, ns),
        in_specs=[
            pl.BlockSpec((None, ts, C_CONV_CH), tile),
            pl.BlockSpec((None, ts, DT_COLS), tile),
            pl.BlockSpec((ts // CHUNK, DT_COLS, CHUNK), lambda b, s: (b * ns + s, 0, 0)),
            pl.BlockSpec((1, DT_COLS), const),
            pl.BlockSpec((DT_COLS, 1), const),
            pl.BlockSpec((1, DT_COLS), const),
            pl.BlockSpec((DT_COLS, 1), const),
            pl.BlockSpec((1, C_WIDTH), const),
        ],
        out_specs=pl.BlockSpec((None, ts, C_WIDTH), tile),
        out_shape=jax.ShapeDtypeStruct((Bn, S, C_WIDTH), BF16),
        scratch_shapes=[pltpu.VMEM((C_STATE, C_WIDTH), F32)],
        compiler_params=pltpu.CompilerParams(
            dimension_semantics=("arbitrary", "arbitrary"), vmem_limit_bytes=VMEM_LIMIT_BYTES),
        name="ssd_fwd",
    )(xact3, dt3, dtT, brow, bcol, arow, acol, dskip)


def _mix_out_kernel(xact_ref, dt_ref, dtT_ref, yf_ref, zc_ref, zab_ref,
                    o1_ref, o4_ref, o16_ref, l1_ref, l4_ref, l16_ref, h_ref,
                    brow_ref, bcol_ref, arow_ref, acol_ref,
                    sguw_ref, sgub_ref, e8_ref, nrm_ref, wout_ref, fnw_ref,
                    out_ref, state_ref, ybuf_ref, mixed_ref, onat_ref, snat_ref, *, ts, final):
    s = pl.program_id(1)

    @pl.when(s == 0)
    def _():
        state_ref[...] = jnp.zeros_like(state_ref)

    slabs = A_WIDTH // LANES
    for ci, (d, o_ref, l_ref) in enumerate(((ATT_CFGS[1][1], o4_ref, l4_ref),
                                            (ATT_CFGS[2][1], o16_ref, l16_ref))):
        for r in range(d):
            dst_rows = pl.ds(r, ts // d, stride=d)
            for j in range(slabs):
                c0 = r * A_WIDTH + j * LANES
                onat_ref[ci, j, dst_rows, :] = o_ref[:, c0:c0 + LANES].astype(F32)
            snat_ref[ci, dst_rows, :] = l_ref[:, r * LANES:(r + 1) * LANES]

    brow, bcol = brow_ref[:, C_HEADS:DT_COLS], bcol_ref[C_HEADS:DT_COLS, :]
    arow, acol = arow_ref[:, C_HEADS:DT_COLS], acol_ref[C_HEADS:DT_COLS, :]
    nch = ts // CHUNK

    def body(i, carry):
        c = nch - 1 - i
        r0 = pl.multiple_of(c * CHUNK, CHUNK)
        rows = pl.ds(r0, CHUNK)

        xs = xact_ref[rows, 0:C_WIDTH]
        bm = xact_ref[rows, C_WIDTH:C_WIDTH + C_GROUPS * C_STATE]
        cm = xact_ref[rows, C_WIDTH + C_GROUPS * C_STATE:C_CONV_CH]
        ys = _ssd_chunk(xs, bm, cm, dt_ref[rows, C_HEADS:DT_COLS], dtT_ref[c, C_HEADS:DT_COLS, :],
                        brow, bcol, arow, acol, state_ref, reverse=True)
        ssq = jnp.zeros((CHUNK, 1), F32)
        for pair, y in enumerate(ys):
            cols = slice(pair * LANES, (pair + 1) * LANES)
            y = (y + yf_ref[rows, cols].astype(F32)) * _silu(zc_ref[rows, cols].astype(F32))
            ybuf_ref[:, cols] = y
            ssq = ssq + jnp.sum(y * y, axis=-1, keepdims=True)
        inv = lax.rsqrt(ssq * (1.0 / C_WIDTH) + EPS)
        mixed_ref[rows, A_WIDTH + B_WIDTH:D_MIX] = (ybuf_ref[...] * inv * nrm_ref[...]).astype(BF16)

        st = [l1_ref[rows, :], snat_ref[0, rows, :], snat_ref[1, rows, :]]
        mx = jnp.maximum(jnp.maximum(st[0], st[1]), st[2])
        es = [jnp.exp2(t - mx) for t in st]
        dens = [pltpu.roll(t, LANES - A_HEADS, 1) for t in st]
        z = es[0] * dens[0] + es[1] * dens[1] + es[2] * dens[2]
        head_lane = lax.broadcasted_iota(jnp.int32, (CHUNK, LANES), 1) < A_HEADS
        rz = 1.0 / jnp.where(head_lane, z, 1.0)
        wexp = [_dot((ec * rz).astype(BF16), e8_ref[...]) for ec in es]
        for j in range(A_WIDTH // LANES):
            cols = slice(j * LANES, (j + 1) * LANES)
            oa = (wexp[0][:, cols] * o1_ref[rows, cols].astype(F32)
                  + wexp[1][:, cols] * onat_ref[0, j, rows, :]
                  + wexp[2][:, cols] * onat_ref[1, j, rows, :])
            za = zab_ref[rows, cols].astype(F32)
            mixed_ref[rows, cols] = (oa * _silu(za)).astype(BF16)

        for g in range(B_GROUPS):
            cols = slice(g * LANES, (g + 1) * LANES)
            ub = zab_ref[rows, A_WIDTH + g * LANES:A_WIDTH + (g + 1) * LANES].astype(F32)
            vb = zab_ref[rows, 2 * A_WIDTH + g * LANES:2 * A_WIDTH + (g + 1) * LANES]
            zb = zab_ref[rows, 3 * A_WIDTH + g * LANES:3 * A_WIDTH + (g + 1) * LANES].astype(F32)
            mixed = _dot(sguw_ref[g], vb) + sgub_ref[:, cols]
            mixed_ref[rows, A_WIDTH + g * LANES:A_WIDTH + (g + 1) * LANES] = (
                ub * mixed * _silu(zb)).astype(BF16)
        return carry

    lax.fori_loop(0, nch, body, 0)

    hn = h_ref[...] + _dot(mixed_ref[...], wout_ref[...])
    if final:
        ms = jnp.mean(hn * hn, axis=-1, keepdims=True)
        hn = hn * lax.rsqrt(ms + EPS) * fnw_ref[...]
    out_ref[...] = hn


def _mix_out(xact3, dt3, dtT, yf3, zc3, zab3, o1, o4, o16, l1, l4, l16, h3,
             brow, bcol, arow, acol, sguw, sgub, e8, nrm, wout, fnw, final):
    Bn, S, _ = xact3.shape
    ts = TS_SEQ
    ns = S // ts
    const2 = lambda b, s: (0, 0)
    const3 = lambda b, s: (0, 0, 0)
    tile = lambda b, s: (b, ns - 1 - s, 0)
    tspec = lambda w: pl.BlockSpec((None, ts, w), tile)
    gspec = lambda d, w: pl.BlockSpec((None, ts // d, d * w), tile)
    d4, d16 = ATT_CFGS[1][1], ATT_CFGS[2][1]
    return pl.pallas_call(
        functools.partial(_mix_out_kernel, ts=ts, final=final),
        grid=(Bn, ns),
        in_specs=[
            tspec(C_CONV_CH), tspec(DT_COLS),
            pl.BlockSpec((ts // CHUNK, DT_COLS, CHUNK), lambda b, s: (b * ns + ns - 1 - s, 0, 0)),
            tspec(C_WIDTH), tspec(C_WIDTH), tspec(2048),
            tspec(A_WIDTH), gspec(d4, A_WIDTH), gspec(d16, A_WIDTH),
            tspec(LANES), gspec(d4, LANES), gspec(d16, LANES),
            tspec(D_MODEL),
            pl.BlockSpec((1, DT_COLS), const2),
            pl.BlockSpec((DT_COLS, 1), const2),
            pl.BlockSpec((1, DT_COLS), const2),
            pl.BlockSpec((DT_COLS, 1), const2),
            pl.BlockSpec((B_GROUPS, CHUNK, CHUNK), const3),
            pl.BlockSpec((CHUNK, B_WIDTH), const2),
            pl.BlockSpec((LANES, A_WIDTH), const2),
            pl.BlockSpec((1, C_WIDTH), const2),
            pl.BlockSpec((D_MIX, D_MODEL), const2),
            pl.BlockSpec((1, D_MODEL), const2),
        ],
        out_specs=pl.BlockSpec((None, ts, D_MODEL), tile),
        out_shape=jax.ShapeDtypeStruct((Bn, S, D_MODEL), F32),
        scratch_shapes=[
            pltpu.VMEM((C_STATE, C_WIDTH), F32),
            pltpu.VMEM((CHUNK, C_WIDTH), F32),
            pltpu.VMEM((ts, D_MIX), BF16),
            pltpu.VMEM((2, A_WIDTH // LANES, ts, LANES), F32),
            pltpu.VMEM((2, ts, LANES), F32),
        ],
        compiler_params=pltpu.CompilerParams(
            dimension_semantics=("arbitrary", "arbitrary"), vmem_limit_bytes=VMEM_LIMIT_BYTES),
        name="mix_out_final" if final else "mix_out",
    )(xact3, dt3, dtT, yf3, zc3, zab3, o1, o4, o16, l1, l4, l16, h3,
      brow, bcol, arow, acol, sguw, sgub, e8, nrm, wout, fnw)


def _rope_tables(S):
    inv = ROPE_THETA ** (-jnp.arange(0, HEAD_DIM, 2, dtype=F32) / HEAD_DIM)
    ang = jnp.arange(S, dtype=F32)[:, None] * inv[None, :]
    cos, sin = jnp.cos(ang), jnp.sin(ang)
    cos_t = jnp.concatenate([cos, cos, cos, cos], axis=-1)
    sin_t = jnp.concatenate([-sin, sin, -sin, sin], axis=-1)
    return cos_t, sin_t


def _layer_params(l, norm_w, w_in, sgu_w, sgu_b, conv_w, conv_b, dt_bias, a_log, d_skip,
                  ssd_norm_w, w_out):
    w = w_in[l]
    head_of_lane = jnp.arange(A_WIDTH) // HEAD_DIM
    e8 = (jnp.arange(LANES)[:, None] == head_of_lane[None, :]).astype(BF16)
    return dict(
        nw=norm_w[l][None, :],
        w_main=w[:, :MAIN_COLS].astype(BF16),
        w_dt=w[:, MAIN_COLS:].astype(BF16),
        w_dtT=w[:, MAIN_COLS:].T.astype(BF16),
        sguw=sgu_w[l].astype(BF16),
        sgub=jnp.repeat(sgu_b[l].T, B_WIDTH // B_GROUPS, axis=1),
        conv_w=conv_w[l],
        conv_b=conv_b[l][None, :],
        brow=dt_bias[l].reshape(1, DT_COLS),
        bcol=dt_bias[l].reshape(DT_COLS, 1),
        arow=a_log[l].reshape(1, DT_COLS),
        acol=a_log[l].reshape(DT_COLS, 1),
        dskip=jnp.repeat(d_skip[l], HEAD_DIM)[None, :],
        nrm=ssd_norm_w[l][None, :],
        wout=w_out[l].astype(BF16),
        e8=e8,
    )


def _trunk(x, layers, fnw):
    Bn, S, _ = x.shape
    T = Bn * S
    assert S % (ATT_Q * max(d for _, d in ATT_CFGS)) == 0 and S % TS_SEQ == 0 and S % TM_PROJ == 0
    cos_t, sin_t = _rope_tables(S)
    h = x
    for li, p in enumerate(layers):
        qkv, qkv4, qkv16, zab, xact, zc, dt, dtT = _in_proj(
            h.reshape(T, D_MODEL), S, p["nw"], cos_t, sin_t, p["w_main"], p["w_dt"], p["w_dtT"],
            p["conv_w"], p["conv_b"])
        att = [_attention(t.reshape(Bn, S // dil, dil * 3 * A_WIDTH), dil)
               for t, (_, dil) in zip((qkv, qkv4, qkv16), ATT_CFGS)]
        xact3 = xact.reshape(Bn, S, C_CONV_CH)
        dt3 = dt.reshape(Bn, S, DT_COLS)
        yf3 = _ssd_fwd(xact3, dt3, dtT, p["brow"], p["bcol"], p["arow"], p["acol"], p["dskip"])
        h = _mix_out(xact3, dt3, dtT, yf3, zc.reshape(Bn, S, C_WIDTH), zab.reshape(Bn, S, 2048),
                     att[0][0], att[1][0], att[2][0], att[0][1], att[1][1], att[2][1], h,
                     p["brow"], p["bcol"], p["arow"], p["acol"], p["sguw"], p["sgub"], p["e8"],
                     p["nrm"], p["wout"], fnw, final=(li == len(layers) - 1))
    return h


def kernel(x_prompt, x_sample, norm_w, w_in, sgu_w, sgu_b, conv_w, conv_b, dt_bias, a_log, d_skip,
           ssd_norm_w, w_out, final_norm_w):
    depth = w_in.shape[0]
    layers = [_layer_params(l, norm_w, w_in, sgu_w, sgu_b, conv_w, conv_b, dt_bias, a_log, d_skip,
                            ssd_norm_w, w_out) for l in range(depth)]
    fnw = final_norm_w[None, :]
    return (_trunk(x_prompt, layers, fnw), _trunk(x_sample, layers, fnw))
```

```python
import functools

import jax
import jax.numpy as jnp
from jax import lax
from jax.experimental import pallas as pl
from jax.experimental.pallas import tpu as pltpu

F32 = jnp.float32
BF16 = jnp.bfloat16

D_MODEL = 1024
HEAD_DIM = 64
A_WIDTH = 512
A_HEADS = 8
ATT_CFGS = ((128, 1), (512, 4), (2048, 16))
ATT_Q = 128
ATT_HALF = 64
ROPE_THETA = 10000.0
B_WIDTH = 512
B_GROUPS = 4
CHUNK = 128
C_WIDTH = 1024
C_HEADS = 16
C_STATE = 128
C_GROUPS = 2
C_CONV_CH = C_WIDTH + 2 * C_GROUPS * C_STATE
MAIN_COLS = 6144
DT_COLS = 2 * C_HEADS
D_MIX = 2048
EPS = 1e-5
NEG = -1e30
LOG2E = 1.4426950408889634
LN2 = 0.6931471805599453

VMEM_LIMIT_BYTES = 56 * 1024 * 1024
LANES = 128
F32_SUBLANES = 8
EDGE_ROWS = 16
MXU_COLS = 256

TM_PROJ = 512
TS_SEQ = 512
TS_SCAN = 1024
ATT_QB = 1024
ATT_SUBQ = 64


def _silu(x):
    hx = 0.5 * x
    return hx + hx * jnp.tanh(hx)


def _softplus(x):
    return jnp.maximum(x, 0.0) + jnp.log1p(jnp.exp(-jnp.abs(x)))


def _split3(x):
    hi = x.astype(BF16)
    r1 = x - hi.astype(F32)
    mid = r1.astype(BF16)
    lo = (r1 - mid.astype(F32)).astype(BF16)
    return hi, mid, lo


def _dot(a, b):
    return jnp.dot(a, b, preferred_element_type=F32)


def _dot_nt(a, b):
    return lax.dot_general(a, b, (((1,), (1,)), ((), ())), preferred_element_type=F32)


def _inproj_kernel(x_ref, xprev_ref, xnext_ref, nw_ref, cos_ref, sin_ref, w_ref, wdt_ref, wdtT_ref,
                   cw_ref, cb_ref,
                   qkv_ref, qkv4_ref, qkv16_ref, zab_ref, xact_ref, zc_ref, dt_ref, dtT_ref,
                   qkv32_ref, by4_ref, *, tm, tiles_per_seq):
    ti = pl.program_id(0) % tiles_per_seq
    d4, d16 = ATT_CFGS[1][1], ATT_CFGS[2][1]
    x = jnp.concatenate([x_ref[...], xprev_ref[...], xnext_ref[...]], axis=0)
    ms = jnp.mean(x * x, axis=-1, keepdims=True)
    xn_ext = (x * lax.rsqrt(ms + EPS) * nw_ref[...]).astype(BF16)
    xn = xn_ext[0:tm, :]
    cos = cos_ref[...]
    sin = sin_ref[...]
    lane = lax.broadcasted_iota(jnp.int32, cos.shape, 1)
    first_half = (lane % HEAD_DIM) < (HEAD_DIM // 2)
    gw = MXU_COLS
    gslabs = gw // LANES
    qk_cols, qkv_cols = 2 * A_WIDTH, 3 * A_WIDTH
    zab_end, xbc_end = qkv_cols + 2048, qkv_cols + 2048 + C_CONV_CH
    erow = lax.broadcasted_iota(jnp.int32, (EDGE_ROWS, gw), 0)

    groups = list(range(0, MAIN_COLS, gw))
    heavy = [c for c in groups if c < qkv_cols or zab_end <= c < xbc_end]
    plain = [c for c in groups if c not in heavy]
    assert len(heavy) == len(plain)
    for c0 in [c for pair in zip(heavy, plain) for c in pair]:
        if c0 < qkv_cols:
            acc = _dot(xn, w_ref[:, c0:c0 + gw])
            scale = HEAD_DIM ** -0.5 * LOG2E if c0 < A_WIDTH else 1.0
            for j in range(gslabs):
                slab = c0 // LANES + j
                out = acc[:, j * LANES:(j + 1) * LANES]
                if c0 < qk_cols:
                    fwd = pltpu.roll(out, HEAD_DIM // 2, 1)
                    bwd = pltpu.roll(out, LANES - HEAD_DIM // 2, 1)
                    out = out * cos + jnp.where(first_half, bwd, fwd) * sin
                    if scale != 1.0:
                        out = out * scale
                qkv32_ref[slab] = out
                qkv_ref[:, slab * LANES:(slab + 1) * LANES] = out.astype(BF16)
                n4 = tm // d4
                for r in range(d4):
                    piece = qkv32_ref[slab, pl.ds(r, n4, stride=d4), :]
                    by4_ref[slab, r * n4:(r + 1) * n4, :] = piece
                    c = r * qkv_cols + slab * LANES
                    qkv4_ref[:, c:c + LANES] = piece.astype(BF16)
                for r in range(d16):
                    piece = by4_ref[slab, pl.ds((r % d4) * n4 + r // d4, tm // d16, stride=d16 // d4), :]
                    c = r * qkv_cols + slab * LANES
                    qkv16_ref[:, c:c + LANES] = piece.astype(BF16)
        elif c0 < zab_end:
            zab_ref[:, c0 - qkv_cols:c0 - qkv_cols + gw] = _dot(xn, w_ref[:, c0:c0 + gw]).astype(BF16)
        elif c0 < xbc_end:
            cols = slice(c0 - zab_end, c0 - zab_end + gw)
            ext = _dot(xn_ext, w_ref[:, c0:c0 + gw])
            cur = ext[0:tm, :]
            prev_row = jnp.where(ti > 0, ext[tm + 7:tm + 8, :], 0.0)
            next_row = jnp.where(ti < tiles_per_seq - 1, ext[tm + 8:tm + 9, :], 0.0)

            def conv_act(xm1, x0, xp1):
                conv = (cw_ref[0:1, cols] * xm1 + cw_ref[1:2, cols] * x0 + cw_ref[2:3, cols] * xp1
                        + cb_ref[:, cols])
                return _silu(conv).astype(BF16)

            xact_ref[:, cols] = conv_act(pltpu.roll(cur, 1, 0), cur, pltpu.roll(cur, tm - 1, 0))
            e = EDGE_ROWS
            top, bot = cur[0:e, :], cur[tm - e:tm, :]
            first, last = erow == 0, erow == e - 1
            top_m1 = jnp.where(first, prev_row, pltpu.roll(top, 1, 0))
            top_p1 = jnp.where(last, cur[e:e + 1, :], pltpu.roll(top, e - 1, 0))
            xact_ref[0:e, cols] = conv_act(top_m1, top, top_p1)
            bot_m1 = jnp.where(first, cur[tm - e - 1:tm - e, :], pltpu.roll(bot, 1, 0))
            bot_p1 = jnp.where(last, next_row, pltpu.roll(bot, e - 1, 0))
            xact_ref[tm - e:tm, cols] = conv_act(bot_m1, bot, bot_p1)
        else:
            zc_ref[:, c0 - xbc_end:c0 - xbc_end + gw] = _dot(xn, w_ref[:, c0:c0 + gw]).astype(BF16)
    dt_ref[...] = _dot(xn, wdt_ref[...])
    dtT = _dot_nt(wdtT_ref[...], xn)
    for c in range(dtT_ref.shape[0]):
        dtT_ref[c] = dtT[:, c * CHUNK:(c + 1) * CHUNK]


def _in_proj(x2d, seq_len, nw, cos_t, sin_t, w_main, w_dt, w_dtT, conv_w, conv_b):
    T = x2d.shape[0]
    tm = TM_PROJ
    tiles_per_seq = seq_len // tm
    hb = tm // F32_SUBLANES
    last_hb = T // F32_SUBLANES - 1
    d4, d16 = ATT_CFGS[1][1], ATT_CFGS[2][1]
    const = lambda i: (0, 0)
    row = lambda i: (i, 0)
    return pl.pallas_call(
        functools.partial(_inproj_kernel, tm=tm, tiles_per_seq=tiles_per_seq),
        grid=(T // tm,),
        in_specs=[
            pl.BlockSpec((tm, D_MODEL), row),
            pl.BlockSpec((F32_SUBLANES, D_MODEL), lambda i: (jnp.maximum(i * hb - 1, 0), 0)),
            pl.BlockSpec((F32_SUBLANES, D_MODEL), lambda i: (jnp.minimum((i + 1) * hb, last_hb), 0)),
            pl.BlockSpec((1, D_MODEL), const),
            pl.BlockSpec((tm, LANES), lambda i: (i % tiles_per_seq, 0)),
            pl.BlockSpec((tm, LANES), lambda i: (i % tiles_per_seq, 0)),
            pl.BlockSpec((D_MODEL, MAIN_COLS), const),
            pl.BlockSpec((D_MODEL, DT_COLS), const),
            pl.BlockSpec((DT_COLS, D_MODEL), const),
            pl.BlockSpec((3, C_CONV_CH), const),
            pl.BlockSpec((1, C_CONV_CH), const),
        ],
        out_specs=[
            pl.BlockSpec((tm, 3 * A_WIDTH), row),
            pl.BlockSpec((tm // d4, d4 * 3 * A_WIDTH), row),
            pl.BlockSpec((tm // d16, d16 * 3 * A_WIDTH), row),
            pl.BlockSpec((tm, 2048), row),
            pl.BlockSpec((tm, C_CONV_CH), row),
            pl.BlockSpec((tm, C_WIDTH), row),
            pl.BlockSpec((tm, DT_COLS), row),
            pl.BlockSpec((tm // CHUNK, DT_COLS, CHUNK), lambda i: (i, 0, 0)),
        ],
        out_shape=[
            jax.ShapeDtypeStruct((T, 3 * A_WIDTH), BF16),
            jax.ShapeDtypeStruct((T // d4, d4 * 3 * A_WIDTH), BF16),
            jax.ShapeDtypeStruct((T // d16, d16 * 3 * A_WIDTH), BF16),
            jax.ShapeDtypeStruct((T, 2048), BF16),
            jax.ShapeDtypeStruct((T, C_CONV_CH), BF16),
            jax.ShapeDtypeStruct((T, C_WIDTH), BF16),
            jax.ShapeDtypeStruct((T, DT_COLS), F32),
            jax.ShapeDtypeStruct((T // CHUNK, DT_COLS, CHUNK), F32),
        ],
        scratch_shapes=[pltpu.VMEM((3 * A_WIDTH // LANES, tm, LANES), F32),
                        pltpu.VMEM((3 * A_WIDTH // LANES, tm, LANES), F32)],
        compiler_params=pltpu.CompilerParams(
            dimension_semantics=("arbitrary",), vmem_limit_bytes=VMEM_LIMIT_BYTES),
        name="in_proj",
    )(x2d, x2d, x2d, nw, cos_t, sin_t, w_main, w_dt, w_dtT, conv_w, conv_b)


def _attn_kernel(q_ref, kp_ref, kc_ref, kn_ref, vp_ref, vc_ref, vn_ref, o_ref, lse_ref,
                 kbuf, vbuf, *, qb, sub_len):
    n = pl.program_id(2)
    kbuf[0:ATT_Q, :] = kp_ref[...]
    kbuf[ATT_Q:ATT_Q + qb, :] = kc_ref[...]
    kbuf[ATT_Q + qb:, :] = kn_ref[...]
    vbuf[0:ATT_Q, :] = vp_ref[...]
    vbuf[ATT_Q:ATT_Q + qb, :] = vc_ref[...]
    vbuf[ATT_Q + qb:, :] = vn_ref[...]

    sq = ATT_SUBQ
    nkeys = MXU_COLS
    assert sq + 2 * ATT_HALF <= nkeys
    qq = lax.broadcasted_iota(jnp.int32, (sq, nkeys), 0)
    kk = lax.broadcasted_iota(jnp.int32, (sq, nkeys), 1)
    kk1 = lax.broadcasted_iota(jnp.int32, (1, nkeys), 1)
    lane = lax.broadcasted_iota(jnp.int32, (sq, LANES), 1)
    low_half = lane < HEAD_DIM
    high_half = jnp.logical_not(low_half)
    band_bias = jnp.where((kk >= qq) & (kk <= qq + 2 * ATT_HALF), 0.0, NEG)
    ones = jnp.ones((nkeys, LANES), BF16)

    for i in range(qb // sq):
        r0 = i * sq
        k0 = i * sq + ATT_Q - ATT_HALF
        q = q_ref[r0:r0 + sq, :]
        kw = kbuf[k0:k0 + nkeys, :]
        vw = vbuf[k0:k0 + nkeys, :]
        pos = kk1 + (n * qb + i * sq - ATT_HALF)
        bias = band_bias + jnp.where((pos >= 0) & (pos < sub_len), 0.0, NEG)
        rows = slice(r0, r0 + sq)
        lse_ref[rows, :] = jnp.zeros((sq, LANES), F32)
        for j in range(A_WIDTH // LANES):
            cols = slice(j * LANES, (j + 1) * LANES)
            qp, kp, vp = q[:, cols], kw[:, cols], vw[:, cols]
            zero = jnp.zeros_like(qp)
            qs = jnp.concatenate([jnp.where(low_half, qp, zero), jnp.where(high_half, qp, zero)], axis=0)
            s = _dot_nt(qs, kp)
            ps = []
            for e in range(2):
                se = s[e * sq:(e + 1) * sq, :] + bias
                m = jnp.max(se, axis=-1, keepdims=True)
                ps.append(jnp.exp2(se - m).astype(BF16))
                lse_ref[rows, 2 * j + e:2 * j + e + 1] = m
            o = _dot(jnp.concatenate(ps, axis=0), jnp.concatenate([vp, ones], axis=1))
            for e in range(2):
                h = A_HEADS + 2 * j + e
                lse_ref[rows, h:h + 1] = o[e * sq:(e + 1) * sq, LANES + h:LANES + h + 1]
            o_ref[rows, cols] = jnp.where(low_half, o[0:sq, 0:LANES], o[sq:, 0:LANES]).astype(BF16)


def _attention(view, dil):
    Bn, L, _ = view.shape
    qb = min(ATT_QB, L)
    nb = L // qb
    r128 = qb // ATT_Q
    last128 = L // ATT_Q - 1

    def cur(c):
        return pl.BlockSpec((None, qb, A_WIDTH), lambda b, r, n: (b, n, 3 * r + c))

    def prev(c):
        return pl.BlockSpec((None, ATT_Q, A_WIDTH),
                            lambda b, r, n: (b, jnp.maximum(n * r128 - 1, 0), 3 * r + c))

    def nxt(c):
        return pl.BlockSpec((None, ATT_Q, A_WIDTH),
                            lambda b, r, n: (b, jnp.minimum((n + 1) * r128, last128), 3 * r + c))

    return pl.pallas_call(
        functools.partial(_attn_kernel, qb=qb, sub_len=L),
        grid=(Bn, dil, nb),
        in_specs=[cur(0), prev(1), cur(1), nxt(1), prev(2), cur(2), nxt(2)],
        out_specs=[
            pl.BlockSpec((None, qb, A_WIDTH), lambda b, r, n: (b, n, r)),
            pl.BlockSpec((None, qb, LANES), lambda b, r, n: (b, n, r)),
        ],
        out_shape=[
            jax.ShapeDtypeStruct((Bn, L, dil * A_WIDTH), BF16),
            jax.ShapeDtypeStruct((Bn, L, dil * LANES), F32),
        ],
        scratch_shapes=[
            pltpu.VMEM((qb + 2 * ATT_Q, A_WIDTH), BF16),
            pltpu.VMEM((qb + 2 * ATT_Q, A_WIDTH), BF16),
        ],
        compiler_params=pltpu.CompilerParams(
            dimension_semantics=("arbitrary", "arbitrary", "arbitrary"),
            vmem_limit_bytes=VMEM_LIMIT_BYTES),
        name=f"attn_d{dil}",
    )(view, view, view, view, view, view, view)


def _ssd_chunk(xs, bm, cm, dt_raw, dt_rawT, bias_row, bias_col, alog_row, alog_col,
               state_ref, reverse):
    L = CHUNK
    dt = _softplus(dt_raw + bias_row)
    dtT = _softplus(dt_rawT + bias_col)
    dtA = dt * (-jnp.exp(alog_row))
    dtAT = dtT * (-jnp.exp(alog_col))
    ri = lax.broadcasted_iota(jnp.int32, (L, L), 0)
    ci = lax.broadcasted_iota(jnp.int32, (L, L), 1)
    tri = (ci >= ri) if reverse else (ci <= ri)
    tri_bf = jnp.where(tri, 1.0, 0.0).astype(BF16)
    acum = sum(_dot(tri_bf, t) for t in _split3(dtA))
    acumT = sum(_dot_nt(t, tri_bf) for t in _split3(dtAT))
    e = 0 if reverse else L - 1
    edge_row = acum[e:e + 1, :]
    edge_col = acumT[:, e:e + 1]
    wT = jnp.exp(edge_col - acumT) * dtT
    cdec = jnp.exp(edge_row)
    a2 = acum * LOG2E
    srcT = acumT * LOG2E - jnp.log2(dtT)
    lane = lax.broadcasted_iota(jnp.int32, (L, LANES), 1)
    low_half = lane < HEAD_DIM
    high_half = jnp.logical_not(low_half)

    def block_diag(t):
        zero = jnp.zeros_like(t)
        return jnp.concatenate([jnp.where(low_half, t, zero), jnp.where(high_half, t, zero)], axis=0)

    ys = []
    for g in range(C_GROUPS):
        bg = bm[:, g * C_STATE:(g + 1) * C_STATE]
        cg = cm[:, g * C_STATE:(g + 1) * C_STATE]
        cb = _dot_nt(cg, bg).astype(BF16)
        bgT = bg.astype(F32).T.astype(BF16)
        for pr in range(4):
            pair = g * 4 + pr
            cols = slice(pair * LANES, (pair + 1) * LANES)
            xbd = block_diag(xs[:, cols])
            prev = state_ref[:, cols]
            pbd = block_diag(prev.astype(BF16))
            ws, css, bss, cd = [], [], [], []
            for k in range(2):
                h = 2 * pair + k
                col = jnp.broadcast_to(a2[:, h:h + 1], (L, L))
                dec = jnp.exp2(jnp.where(tri, col - srcT[h:h + 1, :], NEG))
                ws.append(cb * dec.astype(BF16))
                css.append(cg * jnp.exp2(col).astype(BF16))
                bss.append(bgT * jnp.broadcast_to(wT[h:h + 1, :], (C_STATE, L)).astype(BF16))
                cd.append(cdec[:, h:h + 1])
            ys.append(_dot(jnp.concatenate(ws + css, axis=1), jnp.concatenate([xbd, pbd], axis=0)))
            decay = jnp.where(low_half[0:1, :], cd[0], cd[1])
            state_ref[:, cols] = prev * decay + _dot(jnp.concatenate(bss, axis=1), xbd)
    return ys


def _ssd_fwd_kernel(xact_ref, dt_ref, dtT_ref, brow_ref, bcol_ref, arow_ref, acol_ref, dskip_ref,
                    yf_ref, state_ref, *, ts):
    s = pl.program_id(1)

    @pl.when(s == 0)
    def _():
        state_ref[...] = jnp.zeros_like(state_ref)

    brow, bcol = brow_ref[:, 0:C_HEADS], bcol_ref[0:C_HEADS, :]
    arow, acol = arow_ref[:, 0:C_HEADS], acol_ref[0:C_HEADS, :]

    def body(c, carry):
        r0 = pl.multiple_of(c * CHUNK, CHUNK)
        rows = pl.ds(r0, CHUNK)
        xs = xact_ref[rows, 0:C_WIDTH]
        bm = xact_ref[rows, C_WIDTH:C_WIDTH + C_GROUPS * C_STATE]
        cm = xact_ref[rows, C_WIDTH + C_GROUPS * C_STATE:C_CONV_CH]
        ys = _ssd_chunk(xs, bm, cm, dt_ref[rows, 0:C_HEADS], dtT_ref[c, 0:C_HEADS, :],
                        brow, bcol, arow, acol, state_ref, reverse=False)
        for pair, y in enumerate(ys):
            cols = slice(pair * LANES, (pair + 1) * LANES)
            y = y + dskip_ref[:, cols] * xs[:, cols].astype(F32)
            yf_ref[rows, cols] = y.astype(BF16)
        return carry

    lax.fori_loop(0, ts // CHUNK, body, 0, unroll=2)


def _ssd_fwd(xact3, dt3, dtT, brow, bcol, arow, acol, dskip):
    Bn, S, _ = xact3.shape
    ts = TS_SCAN
    ns = S // ts
    const = lambda b, s: (0, 0)
    tile = lambda b, s: (b, s, 0)
    return pl.pallas_call(
        functools.partial(_ssd_fwd_kernel, ts=ts),
        grid=(Bn, ns),
        in_specs=[
            pl.BlockSpec((None, ts, C_CONV_CH), tile),
            pl.BlockSpec((None, ts, DT_COLS), tile),
            pl.BlockSpec((ts // CHUNK, DT_COLS, CHUNK), lambda b, s: (b * ns + s, 0, 0)),
            pl.BlockSpec((1, DT_COLS), const),
            pl.BlockSpec((DT_COLS, 1), const),
            pl.BlockSpec((1, DT_COLS), const),
            pl.BlockSpec((DT_COLS, 1), const),
            pl.BlockSpec((1, C_WIDTH), const),
        ],
        out_specs=pl.BlockSpec((None, ts, C_WIDTH), tile),
        out_shape=jax.ShapeDtypeStruct((Bn, S, C_WIDTH), BF16),
        scratch_shapes=[pltpu.VMEM((C_STATE, C_WIDTH), F32)],
        compiler_params=pltpu.CompilerParams(
            dimension_semantics=("arbitrary", "arbitrary"), vmem_limit_bytes=VMEM_LIMIT_BYTES),
        name="ssd_fwd",
    )(xact3, dt3, dtT, brow, bcol, arow, acol, dskip)


def _mix_out_kernel(xact_ref, dt_ref, dtT_ref, yf_ref, zc_ref, zab_ref,
                    o1_ref, o4_ref, o16_ref, l1_ref, l4_ref, l16_ref, h_ref,
                    brow_ref, bcol_ref, arow_ref, acol_ref,
                    sguw_ref, sgub_ref, e8_ref, nrm_ref, wout_ref, fnw_ref,
                    out_ref, state_ref, ybuf_ref, mixed_ref, onat_ref, snat_ref, *, ts, final):
    s = pl.program_id(1)

    @pl.when(s == 0)
    def _():
        state_ref[...] = jnp.zeros_like(state_ref)

    slabs = A_WIDTH // LANES
    for ci, (d, o_ref, l_ref) in enumerate(((ATT_CFGS[1][1], o4_ref, l4_ref),
                                            (ATT_CFGS[2][1], o16_ref, l16_ref))):
        for r in range(d):
            dst_rows = pl.ds(r, ts // d, stride=d)
            for j in range(slabs):
                c0 = r * A_WIDTH + j * LANES
                onat_ref[ci, j, dst_rows, :] = o_ref[:, c0:c0 + LANES].astype(F32)
            snat_ref[ci, dst_rows, :] = l_ref[:, r * LANES:(r + 1) * LANES]

    brow, bcol = brow_ref[:, C_HEADS:DT_COLS], bcol_ref[C_HEADS:DT_COLS, :]
    arow, acol = arow_ref[:, C_HEADS:DT_COLS], acol_ref[C_HEADS:DT_COLS, :]
    nch = ts // CHUNK

    def body(i, carry):
        c = nch - 1 - i
        r0 = pl.multiple_of(c * CHUNK, CHUNK)
        rows = pl.ds(r0, CHUNK)

        xs = xact_ref[rows, 0:C_WIDTH]
        bm = xact_ref[rows, C_WIDTH:C_WIDTH + C_GROUPS * C_STATE]
        cm = xact_ref[rows, C_WIDTH + C_GROUPS * C_STATE:C_CONV_CH]
        ys = _ssd_chunk(xs, bm, cm, dt_ref[rows, C_HEADS:DT_COLS], dtT_ref[c, C_HEADS:DT_COLS, :],
                        brow, bcol, arow, acol, state_ref, reverse=True)
        ssq = jnp.zeros((CHUNK, 1), F32)
        for pair, y in enumerate(ys):
            cols = slice(pair * LANES, (pair + 1) * LANES)
            y = (y + yf_ref[rows, cols].astype(F32)) * _silu(zc_ref[rows, cols].astype(F32))
            ybuf_ref[:, cols] = y
            ssq = ssq + jnp.sum(y * y, axis=-1, keepdims=True)
        inv = lax.rsqrt(ssq * (1.0 / C_WIDTH) + EPS)
        mixed_ref[rows, A_WIDTH + B_WIDTH:D_MIX] = (ybuf_ref[...] * inv * nrm_ref[...]).astype(BF16)

        st = [l1_ref[rows, :], snat_ref[0, rows, :], snat_ref[1, rows, :]]
        mx = jnp.maximum(jnp.maximum(st[0], st[1]), st[2])
        es = [jnp.exp2(t - mx) for t in st]
        dens = [pltpu.roll(t, LANES - A_HEADS, 1) for t in st]
        z = es[0] * dens[0] + es[1] * dens[1] + es[2] * dens[2]
        head_lane = lax.broadcasted_iota(jnp.int32, (CHUNK, LANES), 1) < A_HEADS
        rz = 1.0 / jnp.where(head_lane, z, 1.0)
        wexp = [_dot((ec * rz).astype(BF16), e8_ref[...]) for ec in es]
        for j in range(A_WIDTH // LANES):
            cols = slice(j * LANES, (j + 1) * LANES)
            oa = (wexp[0][:, cols] * o1_ref[rows, cols].astype(F32)
                  + wexp[1][:, cols] * onat_ref[0, j, rows, :]
                  + wexp[2][:, cols] * onat_ref[1, j, rows, :])
            za = zab_ref[rows, cols].astype(F32)
            mixed_ref[rows, cols] = (oa * _silu(za)).astype(BF16)

        for g in range(B_GROUPS):
            cols = slice(g * LANES, (g + 1) * LANES)
            ub = zab_ref[rows, A_WIDTH + g * LANES:A_WIDTH + (g + 1) * LANES].astype(F32)
            vb = zab_ref[rows, 2 * A_WIDTH + g * LANES:2 * A_WIDTH + (g + 1) * LANES]
            zb = zab_ref[rows, 3 * A_WIDTH + g * LANES:3 * A_WIDTH + (g + 1) * LANES].astype(F32)
            mixed = _dot(sguw_ref[g], vb) + sgub_ref[:, cols]
            mixed_ref[rows, A_WIDTH + g * LANES:A_WIDTH + (g + 1) * LANES] = (
                ub * mixed * _silu(zb)).astype(BF16)
        return carry

    lax.fori_loop(0, nch, body, 0, unroll=2)

    hn = h_ref[...] + _dot(mixed_ref[...], wout_ref[...])
    if final:
        ms = jnp.mean(hn * hn, axis=-1, keepdims=True)
        hn = hn * lax.rsqrt(ms + EPS) * fnw_ref[...]
    out_ref[...] = hn


def _mix_out(xact3, dt3, dtT, yf3, zc3, zab3, o1, o4, o16, l1, l4, l16, h3,
             brow, bcol, arow, acol, sguw, sgub, e8, nrm, wout, fnw, final):
    Bn, S, _ = xact3.shape
    ts = TS_SEQ
    ns = S // ts
    const2 = lambda b, s: (0, 0)
    const3 = lambda b, s: (0, 0, 0)
    tile = lambda b, s: (b, ns - 1 - s, 0)
    tspec = lambda w: pl.BlockSpec((None, ts, w), tile)
    gspec = lambda d, w: pl.BlockSpec((None, ts // d, d * w), tile)
    d4, d16 = ATT_CFGS[1][1], ATT_CFGS[2][1]
    return pl.pallas_call(
        functools.partial(_mix_out_kernel, ts=ts, final=final),
        grid=(Bn, ns),
        in_specs=[
            tspec(C_CONV_CH), tspec(DT_COLS),
            pl.BlockSpec((ts // CHUNK, DT_COLS, CHUNK), lambda b, s: (b * ns + ns - 1 - s, 0, 0)),
            tspec(C_WIDTH), tspec(C_WIDTH), tspec(2048),
            tspec(A_WIDTH), gspec(d4, A_WIDTH), gspec(d16, A_WIDTH),
            tspec(LANES), gspec(d4, LANES), gspec(d16, LANES),
            tspec(D_MODEL),
            pl.BlockSpec((1, DT_COLS), const2),
            pl.BlockSpec((DT_COLS, 1), const2),
            pl.BlockSpec((1, DT_COLS), const2),
            pl.BlockSpec((DT_COLS, 1), const2),
            pl.BlockSpec((B_GROUPS, CHUNK, CHUNK), const3),
            pl.BlockSpec((CHUNK, B_WIDTH), const2),
            pl.BlockSpec((LANES, A_WIDTH), const2),
            pl.BlockSpec((1, C_WIDTH), const2),
            pl.BlockSpec((D_MIX, D_MODEL), const2),
            pl.BlockSpec((1, D_MODEL), const2),
        ],
        out_specs=pl.BlockSpec((None, ts, D_MODEL), tile),
        out_shape=jax.ShapeDtypeStruct((Bn, S, D_MODEL), F32),
        scratch_shapes=[
            pltpu.VMEM((C_STATE, C_WIDTH), F32),
            pltpu.VMEM((CHUNK, C_WIDTH), F32),
            pltpu.VMEM((ts, D_MIX), BF16),
            pltpu.VMEM((2, A_WIDTH // LANES, ts, LANES), F32),
            pltpu.VMEM((2, ts, LANES), F32),
        ],
        compiler_params=pltpu.CompilerParams(
            dimension_semantics=("arbitrary", "arbitrary"), vmem_limit_bytes=VMEM_LIMIT_BYTES),
        name="mix_out_final" if final else "mix_out",
    )(xact3, dt3, dtT, yf3, zc3, zab3, o1, o4, o16, l1, l4, l16, h3,
      brow, bcol, arow, acol, sguw, sgub, e8, nrm, wout, fnw)


def _rope_tables(S):
    inv = ROPE_THETA ** (-jnp.arange(0, HEAD_DIM, 2, dtype=F32) / HEAD_DIM)
    ang = jnp.arange(S, dtype=F32)[:, None] * inv[None, :]
    cos, sin = jnp.cos(ang), jnp.sin(ang)
    cos_t = jnp.concatenate([cos, cos, cos, cos], axis=-1)
    sin_t = jnp.concatenate([-sin, sin, -sin, sin], axis=-1)
    return cos_t, sin_t


def _layer_params(l, norm_w, w_in, sgu_w, sgu_b, conv_w, conv_b, dt_bias, a_log, d_skip,
                  ssd_norm_w, w_out):
    w = w_in[l]
    head_of_lane = jnp.arange(A_WIDTH) // HEAD_DIM
    e8 = (jnp.arange(LANES)[:, None] == head_of_lane[None, :]).astype(BF16)
    return dict(
        nw=norm_w[l][None, :],
        w_main=w[:, :MAIN_COLS].astype(BF16),
        w_dt=w[:, MAIN_COLS:].astype(BF16),
        w_dtT=w[:, MAIN_COLS:].T.astype(BF16),
        sguw=sgu_w[l].astype(BF16),
        sgub=jnp.repeat(sgu_b[l].T, B_WIDTH // B_GROUPS, axis=1),
        conv_w=conv_w[l],
        conv_b=conv_b[l][None, :],
        brow=dt_bias[l].reshape(1, DT_COLS),
        bcol=dt_bias[l].reshape(DT_COLS, 1),
        arow=a_log[l].reshape(1, DT_COLS),
        acol=a_log[l].reshape(DT_COLS, 1),
        dskip=jnp.repeat(d_skip[l], HEAD_DIM)[None, :],
        nrm=ssd_norm_w[l][None, :],
        wout=w_out[l].astype(BF16),
        e8=e8,
    )


def _trunk(x, layers, fnw):
    Bn, S, _ = x.shape
    T = Bn * S
    assert S % (ATT_Q * max(d for _, d in ATT_CFGS)) == 0
    assert S % TS_SEQ == 0 and S % TS_SCAN == 0 and S % TM_PROJ == 0
    cos_t, sin_t = _rope_tables(S)
    h = x
    for li, p in enumerate(layers):
        qkv, qkv4, qkv16, zab, xact, zc, dt, dtT = _in_proj(
            h.reshape(T, D_MODEL), S, p["nw"], cos_t, sin_t, p["w_main"], p["w_dt"], p["w_dtT"],
            p["conv_w"], p["conv_b"])
        att = [_attention(t.reshape(Bn, S // dil, dil * 3 * A_WIDTH), dil)
               for t, (_, dil) in zip((qkv, qkv4, qkv16), ATT_CFGS)]
        xact3 = xact.reshape(Bn, S, C_CONV_CH)
        dt3 = dt.reshape(Bn, S, DT_COLS)
        yf3 = _ssd_fwd(xact3, dt3, dtT, p["brow"], p["bcol"], p["arow"], p["acol"], p["dskip"])
        h = _mix_out(xact3, dt3, dtT, yf3, zc.reshape(Bn, S, C_WIDTH), zab.reshape(Bn, S, 2048),
                     att[0][0], att[1][0], att[2][0], att[0][1], att[1][1], att[2][1], h,
                     p["brow"], p["bcol"], p["arow"], p["acol"], p["sguw"], p["sgub"], p["e8"],
                     p["nrm"], p["wout"], fnw, final=(li == len(layers) - 1))
    return h


def kernel(x_prompt, x_sample, norm_w, w_in, sgu_w, sgu_b, conv_w, conv_b, dt_bias, a_log, d_skip,
           ssd_norm_w, w_out, final_norm_w):
    depth = w_in.shape[0]
    layers = [_layer_params(l, norm_w, w_in, sgu_w, sgu_b, conv_w, conv_b, dt_bias, a_log, d_skip,
                            ssd_norm_w, w_out) for l in range(depth)]
    fnw = final_norm_w[None, :]
    return (_trunk(x_prompt, layers, fnw), _trunk(x_sample, layers, fnw))
```

```python
import functools

import jax
import jax.numpy as jnp
from jax import lax
from jax.experimental import pallas as pl
from jax.experimental.pallas import tpu as pltpu

F32 = jnp.float32
BF16 = jnp.bfloat16

D_MODEL = 1024
HEAD_DIM = 64
A_WIDTH = 512
A_HEADS = 8
ATT_CFGS = ((128, 1), (512, 4), (2048, 16))
ATT_Q = 128
ATT_HALF = 64
ROPE_THETA = 10000.0
B_WIDTH = 512
B_GROUPS = 4
CHUNK = 128
C_WIDTH = 1024
C_HEADS = 16
C_STATE = 128
C_GROUPS = 2
C_CONV_CH = C_WIDTH + 2 * C_GROUPS * C_STATE
MAIN_COLS = 6144
DT_COLS = 2 * C_HEADS
D_MIX = 2048
EPS = 1e-5
NEG = -1e30
LOG2E = 1.4426950408889634
LN2 = 0.6931471805599453

VMEM_LIMIT_BYTES = 56 * 1024 * 1024
LANES = 128
F32_SUBLANES = 8
EDGE_ROWS = 16
MXU_COLS = 256

TM_PROJ = 512
TS_SEQ = 512
TS_SCAN = 1024
ATT_QB = 1024
ATT_SUBQ = 64


def _silu(x):
    hx = 0.5 * x
    return hx + hx * jnp.tanh(hx)


def _softplus(x):
    return jnp.maximum(x, 0.0) + jnp.log1p(jnp.exp(-jnp.abs(x)))


def _split3(x):
    hi = x.astype(BF16)
    r1 = x - hi.astype(F32)
    mid = r1.astype(BF16)
    lo = (r1 - mid.astype(F32)).astype(BF16)
    return hi, mid, lo


def _dot(a, b):
    return jnp.dot(a, b, preferred_element_type=F32)


def _dot_nt(a, b):
    return lax.dot_general(a, b, (((1,), (1,)), ((), ())), preferred_element_type=F32)


def _inproj_kernel(x_ref, xprev_ref, xnext_ref, nw_ref, cos_ref, sin_ref, w_ref, wdtT_ref,
                   cw_ref, cb_ref,
                   qkv_ref, qkv4_ref, qkv16_ref, zab_ref, xact_ref, zc_ref, dtT_ref,
                   qkv32_ref, by4_ref, *, tm, tiles_per_seq):
    ti = pl.program_id(0) % tiles_per_seq
    d4, d16 = ATT_CFGS[1][1], ATT_CFGS[2][1]
    x = jnp.concatenate([x_ref[...], xprev_ref[...], xnext_ref[...]], axis=0)
    ms = jnp.mean(x * x, axis=-1, keepdims=True)
    xn_ext = (x * lax.rsqrt(ms + EPS) * nw_ref[...]).astype(BF16)
    xn = xn_ext[0:tm, :]
    cos = cos_ref[...]
    sin = sin_ref[...]
    lane = lax.broadcasted_iota(jnp.int32, cos.shape, 1)
    first_half = (lane % HEAD_DIM) < (HEAD_DIM // 2)
    gw = MXU_COLS
    gslabs = gw // LANES
    qk_cols, qkv_cols = 2 * A_WIDTH, 3 * A_WIDTH
    zab_end, xbc_end = qkv_cols + 2048, qkv_cols + 2048 + C_CONV_CH
    erow = lax.broadcasted_iota(jnp.int32, (EDGE_ROWS, gw), 0)

    groups = list(range(0, MAIN_COLS, gw))
    heavy = [c for c in groups if c < qkv_cols or zab_end <= c < xbc_end]
    plain = [c for c in groups if c not in heavy]
    assert len(heavy) == len(plain)
    for c0 in [c for pair in zip(heavy, plain) for c in pair]:
        if c0 < qkv_cols:
            acc = _dot(xn, w_ref[:, c0:c0 + gw])
            scale = HEAD_DIM ** -0.5 * LOG2E if c0 < A_WIDTH else 1.0
            for j in range(gslabs):
                slab = c0 // LANES + j
                out = acc[:, j * LANES:(j + 1) * LANES]
                if c0 < qk_cols:
                    fwd = pltpu.roll(out, HEAD_DIM // 2, 1)
                    bwd = pltpu.roll(out, LANES - HEAD_DIM // 2, 1)
                    out = out * cos + jnp.where(first_half, bwd, fwd) * sin
                    if scale != 1.0:
                        out = out * scale
                qkv32_ref[slab] = out
                qkv_ref[:, slab * LANES:(slab + 1) * LANES] = out.astype(BF16)
                n4 = tm // d4
                for r in range(d4):
                    piece = qkv32_ref[slab, pl.ds(r, n4, stride=d4), :]
                    by4_ref[slab, r * n4:(r + 1) * n4, :] = piece
                    c = r * qkv_cols + slab * LANES
                    qkv4_ref[:, c:c + LANES] = piece.astype(BF16)
                for r in range(d16):
                    piece = by4_ref[slab, pl.ds((r % d4) * n4 + r // d4, tm // d16, stride=d16 // d4), :]
                    c = r * qkv_cols + slab * LANES
                    qkv16_ref[:, c:c + LANES] = piece.astype(BF16)
        elif c0 < zab_end:
            zab_ref[:, c0 - qkv_cols:c0 - qkv_cols + gw] = _dot(xn, w_ref[:, c0:c0 + gw]).astype(BF16)
        elif c0 < xbc_end:
            cols = slice(c0 - zab_end, c0 - zab_end + gw)
            ext = _dot(xn_ext, w_ref[:, c0:c0 + gw])
            cur = ext[0:tm, :]
            prev_row = jnp.where(ti > 0, ext[tm + 7:tm + 8, :], 0.0)
            next_row = jnp.where(ti < tiles_per_seq - 1, ext[tm + 8:tm + 9, :], 0.0)

            def conv_act(xm1, x0, xp1):
                conv = (cw_ref[0:1, cols] * xm1 + cw_ref[1:2, cols] * x0 + cw_ref[2:3, cols] * xp1
                        + cb_ref[:, cols])
                return _silu(conv).astype(BF16)

            xact_ref[:, cols] = conv_act(pltpu.roll(cur, 1, 0), cur, pltpu.roll(cur, tm - 1, 0))
            e = EDGE_ROWS
            top, bot = cur[0:e, :], cur[tm - e:tm, :]
            first, last = erow == 0, erow == e - 1
            top_m1 = jnp.where(first, prev_row, pltpu.roll(top, 1, 0))
            top_p1 = jnp.where(last, cur[e:e + 1, :], pltpu.roll(top, e - 1, 0))
            xact_ref[0:e, cols] = conv_act(top_m1, top, top_p1)
            bot_m1 = jnp.where(first, cur[tm - e - 1:tm - e, :], pltpu.roll(bot, 1, 0))
            bot_p1 = jnp.where(last, next_row, pltpu.roll(bot, e - 1, 0))
            xact_ref[tm - e:tm, cols] = conv_act(bot_m1, bot, bot_p1)
        else:
            zc_ref[:, c0 - xbc_end:c0 - xbc_end + gw] = _dot(xn, w_ref[:, c0:c0 + gw]).astype(BF16)
    dtT = _dot_nt(wdtT_ref[...], xn)
    for c in range(dtT_ref.shape[0]):
        dtT_ref[c] = dtT[:, c * CHUNK:(c + 1) * CHUNK]


def _in_proj(x2d, seq_len, nw, cos_t, sin_t, w_main, w_dtT, conv_w, conv_b):
    T = x2d.shape[0]
    tm = TM_PROJ
    tiles_per_seq = seq_len // tm
    hb = tm // F32_SUBLANES
    last_hb = T // F32_SUBLANES - 1
    d4, d16 = ATT_CFGS[1][1], ATT_CFGS[2][1]
    const = lambda i: (0, 0)
    row = lambda i: (i, 0)
    return pl.pallas_call(
        functools.partial(_inproj_kernel, tm=tm, tiles_per_seq=tiles_per_seq),
        grid=(T // tm,),
        in_specs=[
            pl.BlockSpec((tm, D_MODEL), row),
            pl.BlockSpec((F32_SUBLANES, D_MODEL), lambda i: (jnp.maximum(i * hb - 1, 0), 0)),
            pl.BlockSpec((F32_SUBLANES, D_MODEL), lambda i: (jnp.minimum((i + 1) * hb, last_hb), 0)),
            pl.BlockSpec((1, D_MODEL), const),
            pl.BlockSpec((tm, LANES), lambda i: (i % tiles_per_seq, 0)),
            pl.BlockSpec((tm, LANES), lambda i: (i % tiles_per_seq, 0)),
            pl.BlockSpec((D_MODEL, MAIN_COLS), const),
            pl.BlockSpec((DT_COLS, D_MODEL), const),
            pl.BlockSpec((3, C_CONV_CH), const),
            pl.BlockSpec((1, C_CONV_CH), const),
        ],
        out_specs=[
            pl.BlockSpec((tm, 3 * A_WIDTH), row),
            pl.BlockSpec((tm // d4, d4 * 3 * A_WIDTH), row),
            pl.BlockSpec((tm // d16, d16 * 3 * A_WIDTH), row),
            pl.BlockSpec((tm, 2048), row),
            pl.BlockSpec((tm, C_CONV_CH), row),
            pl.BlockSpec((tm, C_WIDTH), row),
            pl.BlockSpec((tm // CHUNK, DT_COLS, CHUNK), lambda i: (i, 0, 0)),
        ],
        out_shape=[
            jax.ShapeDtypeStruct((T, 3 * A_WIDTH), BF16),
            jax.ShapeDtypeStruct((T // d4, d4 * 3 * A_WIDTH), BF16),
            jax.ShapeDtypeStruct((T // d16, d16 * 3 * A_WIDTH), BF16),
            jax.ShapeDtypeStruct((T, 2048), BF16),
            jax.ShapeDtypeStruct((T, C_CONV_CH), BF16),
            jax.ShapeDtypeStruct((T, C_WIDTH), BF16),
            jax.ShapeDtypeStruct((T // CHUNK, DT_COLS, CHUNK), F32),
        ],
        scratch_shapes=[pltpu.VMEM((3 * A_WIDTH // LANES, tm, LANES), F32),
                        pltpu.VMEM((3 * A_WIDTH // LANES, tm, LANES), F32)],
        compiler_params=pltpu.CompilerParams(
            dimension_semantics=("arbitrary",), vmem_limit_bytes=VMEM_LIMIT_BYTES),
        name="in_proj",
    )(x2d, x2d, x2d, nw, cos_t, sin_t, w_main, w_dtT, conv_w, conv_b)


def _attn_kernel(q_ref, kp_ref, kc_ref, kn_ref, vp_ref, vc_ref, vn_ref, o_ref, lse_ref,
                 kbuf, vbuf, *, qb, sub_len):
    n = pl.program_id(2)
    kbuf[0:ATT_Q, :] = kp_ref[...]
    kbuf[ATT_Q:ATT_Q + qb, :] = kc_ref[...]
    kbuf[ATT_Q + qb:, :] = kn_ref[...]
    vbuf[0:ATT_Q, :] = vp_ref[...]
    vbuf[ATT_Q:ATT_Q + qb, :] = vc_ref[...]
    vbuf[ATT_Q + qb:, :] = vn_ref[...]

    sq = ATT_SUBQ
    nkeys = MXU_COLS
    assert sq + 2 * ATT_HALF <= nkeys
    qq = lax.broadcasted_iota(jnp.int32, (sq, nkeys), 0)
    kk = lax.broadcasted_iota(jnp.int32, (sq, nkeys), 1)
    kk1 = lax.broadcasted_iota(jnp.int32, (1, nkeys), 1)
    lane = lax.broadcasted_iota(jnp.int32, (sq, LANES), 1)
    low_half = lane < HEAD_DIM
    high_half = jnp.logical_not(low_half)
    band_bias = jnp.where((kk >= qq) & (kk <= qq + 2 * ATT_HALF), 0.0, NEG)
    ones = jnp.ones((nkeys, LANES), BF16)

    for i in range(qb // sq):
        r0 = i * sq
        k0 = i * sq + ATT_Q - ATT_HALF
        q = q_ref[r0:r0 + sq, :]
        kw = kbuf[k0:k0 + nkeys, :]
        vw = vbuf[k0:k0 + nkeys, :]
        pos = kk1 + (n * qb + i * sq - ATT_HALF)
        bias = band_bias + jnp.where((pos >= 0) & (pos < sub_len), 0.0, NEG)
        rows = slice(r0, r0 + sq)
        lse_ref[rows, :] = jnp.zeros((sq, LANES), F32)
        for j in range(A_WIDTH // LANES):
            cols = slice(j * LANES, (j + 1) * LANES)
            qp, kp, vp = q[:, cols], kw[:, cols], vw[:, cols]
            zero = jnp.zeros_like(qp)
            qs = jnp.concatenate([jnp.where(low_half, qp, zero), jnp.where(high_half, qp, zero)], axis=0)
            s = _dot_nt(qs, kp)
            ps = []
            for e in range(2):
                se = s[e * sq:(e + 1) * sq, :] + bias
                m = jnp.max(se, axis=-1, keepdims=True)
                ps.append(jnp.exp2(se - m).astype(BF16))
                lse_ref[rows, 2 * j + e:2 * j + e + 1] = m
            o = _dot(jnp.concatenate(ps, axis=0), jnp.concatenate([vp, ones], axis=1))
            for e in range(2):
                h = A_HEADS + 2 * j + e
                lse_ref[rows, h:h + 1] = o[e * sq:(e + 1) * sq, LANES + h:LANES + h + 1]
            o_ref[rows, cols] = jnp.where(low_half, o[0:sq, 0:LANES], o[sq:, 0:LANES]).astype(BF16)


def _attention(view, dil):
    Bn, L, _ = view.shape
    qb = min(ATT_QB, L)
    nb = L // qb
    r128 = qb // ATT_Q
    last128 = L // ATT_Q - 1

    def cur(c):
        return pl.BlockSpec((None, qb, A_WIDTH), lambda b, r, n: (b, n, 3 * r + c))

    def prev(c):
        return pl.BlockSpec((None, ATT_Q, A_WIDTH),
                            lambda b, r, n: (b, jnp.maximum(n * r128 - 1, 0), 3 * r + c))

    def nxt(c):
        return pl.BlockSpec((None, ATT_Q, A_WIDTH),
                            lambda b, r, n: (b, jnp.minimum((n + 1) * r128, last128), 3 * r + c))

    return pl.pallas_call(
        functools.partial(_attn_kernel, qb=qb, sub_len=L),
        grid=(Bn, dil, nb),
        in_specs=[cur(0), prev(1), cur(1), nxt(1), prev(2), cur(2), nxt(2)],
        out_specs=[
            pl.BlockSpec((None, qb, A_WIDTH), lambda b, r, n: (b, n, r)),
            pl.BlockSpec((None, qb, LANES), lambda b, r, n: (b, n, r)),
        ],
        out_shape=[
            jax.ShapeDtypeStruct((Bn, L, dil * A_WIDTH), BF16),
            jax.ShapeDtypeStruct((Bn, L, dil * LANES), F32),
        ],
        scratch_shapes=[
            pltpu.VMEM((qb + 2 * ATT_Q, A_WIDTH), BF16),
            pltpu.VMEM((qb + 2 * ATT_Q, A_WIDTH), BF16),
        ],
        compiler_params=pltpu.CompilerParams(
            dimension_semantics=("arbitrary", "arbitrary", "arbitrary"),
            vmem_limit_bytes=VMEM_LIMIT_BYTES),
        name=f"attn_d{dil}",
    )(view, view, view, view, view, view, view)


def _ssd_chunk(xs, bm, cm, dt_rawT, bias_col, alog_col, state_ref, reverse):
    L = CHUNK
    dtT = _softplus(dt_rawT + bias_col)
    dtAT = dtT * (-jnp.exp(alog_col))
    ri = lax.broadcasted_iota(jnp.int32, (L, L), 0)
    ci = lax.broadcasted_iota(jnp.int32, (L, L), 1)
    tri = (ci >= ri) if reverse else (ci <= ri)
    tri_bf = jnp.where(tri, 1.0, 0.0).astype(BF16)
    acumT = sum(_dot_nt(t, tri_bf) for t in _split3(dtAT))
    e = 0 if reverse else L - 1
    edge_col = acumT[:, e:e + 1]
    wT = jnp.exp(edge_col - acumT) * dtT
    cdec = jnp.exp(edge_col)
    a2T = acumT * LOG2E
    a2 = a2T.T
    srcT = a2T - jnp.log2(dtT)
    lane = lax.broadcasted_iota(jnp.int32, (L, LANES), 1)
    low_half = lane < HEAD_DIM
    high_half = jnp.logical_not(low_half)

    def block_diag(t):
        zero = jnp.zeros_like(t)
        return jnp.concatenate([jnp.where(low_half, t, zero), jnp.where(high_half, t, zero)], axis=0)

    ys = []
    for g in range(C_GROUPS):
        bg = bm[:, g * C_STATE:(g + 1) * C_STATE]
        cg = cm[:, g * C_STATE:(g + 1) * C_STATE]
        cb = _dot_nt(cg, bg).astype(BF16)
        bgT = bg.astype(F32).T.astype(BF16)
        for pr in range(4):
            pair = g * 4 + pr
            cols = slice(pair * LANES, (pair + 1) * LANES)
            xbd = block_diag(xs[:, cols])
            prev = state_ref[:, cols]
            pbd = block_diag(prev.astype(BF16))
            ws, css, bss, cd = [], [], [], []
            for k in range(2):
                h = 2 * pair + k
                col = jnp.broadcast_to(a2[:, h:h + 1], (L, L))
                dec = jnp.exp2(jnp.where(tri, col - srcT[h:h + 1, :], NEG))
                ws.append(cb * dec.astype(BF16))
                css.append(cg * jnp.exp2(col).astype(BF16))
                bss.append(bgT * jnp.broadcast_to(wT[h:h + 1, :], (C_STATE, L)).astype(BF16))
                cd.append(cdec[h:h + 1, :])
            ys.append(_dot(jnp.concatenate(ws + css, axis=1), jnp.concatenate([xbd, pbd], axis=0)))
            decay = jnp.where(low_half[0:1, :], cd[0], cd[1])
            state_ref[:, cols] = prev * decay + _dot(jnp.concatenate(bss, axis=1), xbd)
    return ys


def _ssd_fwd_kernel(xact_ref, dtT_ref, bcol_ref, acol_ref, dskip_ref, yf_ref, state_ref, *, ts):
    s = pl.program_id(1)

    @pl.when(s == 0)
    def _():
        state_ref[...] = jnp.zeros_like(state_ref)

    bcol, acol = bcol_ref[0:C_HEADS, :], acol_ref[0:C_HEADS, :]

    def body(c, carry):
        r0 = pl.multiple_of(c * CHUNK, CHUNK)
        rows = pl.ds(r0, CHUNK)
        xs = xact_ref[rows, 0:C_WIDTH]
        bm = xact_ref[rows, C_WIDTH:C_WIDTH + C_GROUPS * C_STATE]
        cm = xact_ref[rows, C_WIDTH + C_GROUPS * C_STATE:C_CONV_CH]
        ys = _ssd_chunk(xs, bm, cm, dtT_ref[c, 0:C_HEADS, :], bcol, acol, state_ref, reverse=False)
        for pair, y in enumerate(ys):
            cols = slice(pair * LANES, (pair + 1) * LANES)
            y = y + dskip_ref[:, cols] * xs[:, cols].astype(F32)
            yf_ref[rows, cols] = y.astype(BF16)
        return carry

    lax.fori_loop(0, ts // CHUNK, body, 0, unroll=2)


def _ssd_fwd(xact3, dtT, bcol, acol, dskip):
    Bn, S, _ = xact3.shape
    ts = TS_SCAN
    ns = S // ts
    const = lambda b, s: (0, 0)
    tile = lambda b, s: (b, s, 0)
    return pl.pallas_call(
        functools.partial(_ssd_fwd_kernel, ts=ts),
        grid=(Bn, ns),
        in_specs=[
            pl.BlockSpec((None, ts, C_CONV_CH), tile),
            pl.BlockSpec((ts // CHUNK, DT_COLS, CHUNK), lambda b, s: (b * ns + s, 0, 0)),
            pl.BlockSpec((DT_COLS, 1), const),
            pl.BlockSpec((DT_COLS, 1), const),
            pl.BlockSpec((1, C_WIDTH), const),
        ],
        out_specs=pl.BlockSpec((None, ts, C_WIDTH), tile),
        out_shape=jax.ShapeDtypeStruct((Bn, S, C_WIDTH), BF16),
        scratch_shapes=[pltpu.VMEM((C_STATE, C_WIDTH), F32)],
        compiler_params=pltpu.CompilerParams(
            dimension_semantics=("arbitrary", "arbitrary"), vmem_limit_bytes=VMEM_LIMIT_BYTES),
        name="ssd_fwd",
    )(xact3, dtT, bcol, acol, dskip)


def _mix_out_kernel(xact_ref, dtT_ref, yf_ref, zc_ref, zab_ref,
                    o1_ref, o4_ref, o16_ref, l1_ref, l4_ref, l16_ref, h_ref,
                    bcol_ref, acol_ref,
                    sguw_ref, sgub_ref, e8_ref, nrm_ref, wout_ref, fnw_ref,
                    out_ref, state_ref, ybuf_ref, mixed_ref, onat_ref, snat_ref, *, ts, final):
    s = pl.program_id(1)

    @pl.when(s == 0)
    def _():
        state_ref[...] = jnp.zeros_like(state_ref)

    slabs = A_WIDTH // LANES
    for ci, (d, o_ref, l_ref) in enumerate(((ATT_CFGS[1][1], o4_ref, l4_ref),
                                            (ATT_CFGS[2][1], o16_ref, l16_ref))):
        for r in range(d):
            dst_rows = pl.ds(r, ts // d, stride=d)
            for j in range(slabs):
                c0 = r * A_WIDTH + j * LANES
                onat_ref[ci, j, dst_rows, :] = o_ref[:, c0:c0 + LANES].astype(F32)
            snat_ref[ci, dst_rows, :] = l_ref[:, r * LANES:(r + 1) * LANES]

    bcol, acol = bcol_ref[C_HEADS:DT_COLS, :], acol_ref[C_HEADS:DT_COLS, :]
    nch = ts // CHUNK

    def body(i, carry):
        c = nch - 1 - i
        r0 = pl.multiple_of(c * CHUNK, CHUNK)
        rows = pl.ds(r0, CHUNK)

        xs = xact_ref[rows, 0:C_WIDTH]
        bm = xact_ref[rows, C_WIDTH:C_WIDTH + C_GROUPS * C_STATE]
        cm = xact_ref[rows, C_WIDTH + C_GROUPS * C_STATE:C_CONV_CH]
        ys = _ssd_chunk(xs, bm, cm, dtT_ref[c, C_HEADS:DT_COLS, :], bcol, acol, state_ref, reverse=True)
        ssq = jnp.zeros((CHUNK, 1), F32)
        for pair, y in enumerate(ys):
            cols = slice(pair * LANES, (pair + 1) * LANES)
            y = (y + yf_ref[rows, cols].astype(F32)) * _silu(zc_ref[rows, cols].astype(F32))
            ybuf_ref[:, cols] = y
            ssq = ssq + jnp.sum(y * y, axis=-1, keepdims=True)
        inv = lax.rsqrt(ssq * (1.0 / C_WIDTH) + EPS)
        mixed_ref[rows, A_WIDTH + B_WIDTH:D_MIX] = (ybuf_ref[...] * inv * nrm_ref[...]).astype(BF16)

        st = [l1_ref[rows, :], snat_ref[0, rows, :], snat_ref[1, rows, :]]
        mx = jnp.maximum(jnp.maximum(st[0], st[1]), st[2])
        es = [jnp.exp2(t - mx) for t in st]
        dens = [pltpu.roll(t, LANES - A_HEADS, 1) for t in st]
        z = es[0] * dens[0] + es[1] * dens[1] + es[2] * dens[2]
        head_lane = lax.broadcasted_iota(jnp.int32, (CHUNK, LANES), 1) < A_HEADS
        rz = 1.0 / jnp.where(head_lane, z, 1.0)
        wexp = [_dot((ec * rz).astype(BF16), e8_ref[...]) for ec in es]
        for j in range(A_WIDTH // LANES):
            cols = slice(j * LANES, (j + 1) * LANES)
            oa = (wexp[0][:, cols] * o1_ref[rows, cols].astype(F32)
                  + wexp[1][:, cols] * onat_ref[0, j, rows, :]
                  + wexp[2][:, cols] * onat_ref[1, j, rows, :])
            za = zab_ref[rows, cols].astype(F32)
            mixed_ref[rows, cols] = (oa * _silu(za)).astype(BF16)

        for g in range(B_GROUPS):
            cols = slice(g * LANES, (g + 1) * LANES)
            ub = zab_ref[rows, A_WIDTH + g * LANES:A_WIDTH + (g + 1) * LANES].astype(F32)
            vb = zab_ref[rows, 2 * A_WIDTH + g * LANES:2 * A_WIDTH + (g + 1) * LANES]
            zb = zab_ref[rows, 3 * A_WIDTH + g * LANES:3 * A_WIDTH + (g + 1) * LANES].astype(F32)
            mixed = _dot(sguw_ref[g], vb) + sgub_ref[:, cols]
            mixed_ref[rows, A_WIDTH + g * LANES:A_WIDTH + (g + 1) * LANES] = (
                ub * mixed * _silu(zb)).astype(BF16)
        return carry

    lax.fori_loop(0, nch, body, 0, unroll=2)

    hn = h_ref[...] + _dot(mixed_ref[...], wout_ref[...])
    if final:
        ms = jnp.mean(hn * hn, axis=-1, keepdims=True)
        hn = hn * lax.rsqrt(ms + EPS) * fnw_ref[...]
    out_ref[...] = hn


def _mix_out(xact3, dtT, yf3, zc3, zab3, o1, o4, o16, l1, l4, l16, h3,
             bcol, acol, sguw, sgub, e8, nrm, wout, fnw, final):
    Bn, S, _ = xact3.shape
    ts = TS_SEQ
    ns = S // ts
    const2 = lambda b, s: (0, 0)
    const3 = lambda b, s: (0, 0, 0)
    tile = lambda b, s: (b, ns - 1 - s, 0)
    tspec = lambda w: pl.BlockSpec((None, ts, w), tile)
    gspec = lambda d, w: pl.BlockSpec((None, ts // d, d * w), tile)
    d4, d16 = ATT_CFGS[1][1], ATT_CFGS[2][1]
    return pl.pallas_call(
        functools.partial(_mix_out_kernel, ts=ts, final=final),
        grid=(Bn, ns),
        in_specs=[
            tspec(C_CONV_CH),
            pl.BlockSpec((ts // CHUNK, DT_COLS, CHUNK), lambda b, s: (b * ns + ns - 1 - s, 0, 0)),
            tspec(C_WIDTH), tspec(C_WIDTH), tspec(2048),
            tspec(A_WIDTH), gspec(d4, A_WIDTH), gspec(d16, A_WIDTH),
            tspec(LANES), gspec(d4, LANES), gspec(d16, LANES),
            tspec(D_MODEL),
            pl.BlockSpec((DT_COLS, 1), const2),
            pl.BlockSpec((DT_COLS, 1), const2),
            pl.BlockSpec((B_GROUPS, CHUNK, CHUNK), const3),
            pl.BlockSpec((CHUNK, B_WIDTH), const2),
            pl.BlockSpec((LANES, A_WIDTH), const2),
            pl.BlockSpec((1, C_WIDTH), const2),
            pl.BlockSpec((D_MIX, D_MODEL), const2),
            pl.BlockSpec((1, D_MODEL), const2),
        ],
        out_specs=pl.BlockSpec((None, ts, D_MODEL), tile),
        out_shape=jax.ShapeDtypeStruct((Bn, S, D_MODEL), F32),
        scratch_shapes=[
            pltpu.VMEM((C_STATE, C_WIDTH), F32),
            pltpu.VMEM((CHUNK, C_WIDTH), F32),
            pltpu.VMEM((ts, D_MIX), BF16),
            pltpu.VMEM((2, A_WIDTH // LANES, ts, LANES), F32),
            pltpu.VMEM((2, ts, LANES), F32),
        ],
        compiler_params=pltpu.CompilerParams(
            dimension_semantics=("arbitrary", "arbitrary"), vmem_limit_bytes=VMEM_LIMIT_BYTES),
        name="mix_out_final" if final else "mix_out",
    )(xact3, dtT, yf3, zc3, zab3, o1, o4, o16, l1, l4, l16, h3,
      bcol, acol, sguw, sgub, e8, nrm, wout, fnw)


def _rope_tables(S):
    inv = ROPE_THETA ** (-jnp.arange(0, HEAD_DIM, 2, dtype=F32) / HEAD_DIM)
    ang = jnp.arange(S, dtype=F32)[:, None] * inv[None, :]
    cos, sin = jnp.cos(ang), jnp.sin(ang)
    cos_t = jnp.concatenate([cos, cos, cos, cos], axis=-1)
    sin_t = jnp.concatenate([-sin, sin, -sin, sin], axis=-1)
    return cos_t, sin_t


def _layer_params(l, norm_w, w_in, sgu_w, sgu_b, conv_w, conv_b, dt_bias, a_log, d_skip,
                  ssd_norm_w, w_out):
    w = w_in[l]
    head_of_lane = jnp.arange(A_WIDTH) // HEAD_DIM
    e8 = (jnp.arange(LANES)[:, None] == head_of_lane[None, :]).astype(BF16)
    return dict(
        nw=norm_w[l][None, :],
        w_main=w[:, :MAIN_COLS].astype(BF16),
        w_dtT=w[:, MAIN_COLS:].T.astype(BF16),
        sguw=sgu_w[l].astype(BF16),
        sgub=jnp.repeat(sgu_b[l].T, B_WIDTH // B_GROUPS, axis=1),
        conv_w=conv_w[l],
        conv_b=conv_b[l][None, :],
        bcol=dt_bias[l].reshape(DT_COLS, 1),
        acol=a_log[l].reshape(DT_COLS, 1),
        dskip=jnp.repeat(d_skip[l], HEAD_DIM)[None, :],
        nrm=ssd_norm_w[l][None, :],
        wout=w_out[l].astype(BF16),
        e8=e8,
    )


def _trunk(x, layers, fnw):
    Bn, S, _ = x.shape
    T = Bn * S
    assert S % (ATT_Q * max(d for _, d in ATT_CFGS)) == 0
    assert S % TS_SEQ == 0 and S % TS_SCAN == 0 and S % TM_PROJ == 0
    cos_t, sin_t = _rope_tables(S)
    h = x
    for li, p in enumerate(layers):
        qkv, qkv4, qkv16, zab, xact, zc, dtT = _in_proj(
            h.reshape(T, D_MODEL), S, p["nw"], cos_t, sin_t, p["w_main"], p["w_dtT"],
            p["conv_w"], p["conv_b"])
        att = [_attention(t.reshape(Bn, S // dil, dil * 3 * A_WIDTH), dil)
               for t, (_, dil) in zip((qkv, qkv4, qkv16), ATT_CFGS)]
        xact3 = xact.reshape(Bn, S, C_CONV_CH)
        yf3 = _ssd_fwd(xact3, dtT, p["bcol"], p["acol"], p["dskip"])
        h = _mix_out(xact3, dtT, yf3, zc.reshape(Bn, S, C_WIDTH), zab.reshape(Bn, S, 2048),
                     att[0][0], att[1][0], att[2][0], att[0][1], att[1][1], att[2][1], h,
                     p["bcol"], p["acol"], p["sguw"], p["sgub"], p["e8"],
                     p["nrm"], p["wout"], fnw, final=(li == len(layers) - 1))
    return h


def kernel(x_prompt, x_sample, norm_w, w_in, sgu_w, sgu_b, conv_w, conv_b, dt_bias, a_log, d_skip,
           ssd_norm_w, w_out, final_norm_w):
    depth = w_in.shape[0]
    layers = [_layer_params(l, norm_w, w_in, sgu_w, sgu_b, conv_w, conv_b, dt_bias, a_log, d_skip,
                            ssd_norm_w, w_out) for l in range(depth)]
    fnw = final_norm_w[None, :]
    return (_trunk(x_prompt, layers, fnw), _trunk(x_sample, layers, fnw))
```

```python
import functools

import jax
import jax.numpy as jnp
from jax import lax
from jax.experimental import pallas as pl
from jax.experimental.pallas import tpu as pltpu

F32 = jnp.float32
BF16 = jnp.bfloat16

D_MODEL = 1024
HEAD_DIM = 64
A_WIDTH = 512
A_HEADS = 8
ATT_CFGS = ((128, 1), (512, 4), (2048, 16))
ATT_Q = 128
ATT_HALF = 64
ROPE_THETA = 10000.0
B_WIDTH = 512
B_GROUPS = 4
CHUNK = 128
C_WIDTH = 1024
C_HEADS = 16
C_STATE = 128
C_GROUPS = 2
C_CONV_CH = C_WIDTH + 2 * C_GROUPS * C_STATE
MAIN_COLS = 6144
DT_COLS = 2 * C_HEADS
D_MIX = 2048
EPS = 1e-5
NEG = -1e30
LOG2E = 1.4426950408889634
LN2 = 0.6931471805599453

VMEM_LIMIT_BYTES = 56 * 1024 * 1024
LANES = 128
F32_SUBLANES = 8
EDGE_ROWS = 16
MXU_COLS = 256

TM_PROJ = 512
TS_SEQ = 512
TS_SCAN = 1024
ATT_QB = 1024
ATT_SUBQ = 64


def _silu(x):
    hx = 0.5 * x
    return hx + hx * jnp.tanh(hx)


def _softplus(x):
    return jnp.maximum(x, 0.0) + jnp.log1p(jnp.exp(-jnp.abs(x)))


def _split3(x):
    hi = x.astype(BF16)
    r1 = x - hi.astype(F32)
    mid = r1.astype(BF16)
    lo = (r1 - mid.astype(F32)).astype(BF16)
    return hi, mid, lo


def _dot(a, b):
    return jnp.dot(a, b, preferred_element_type=F32)


def _dot_nt(a, b):
    return lax.dot_general(a, b, (((1,), (1,)), ((), ())), preferred_element_type=F32)


def _inproj_kernel(x_ref, xprev_ref, xnext_ref, nw_ref, cos_ref, sin_ref, w_ref, wdtT_ref,
                   cw_ref, cb_ref,
                   qkv_ref, qkv4_ref, qkv16_ref, zab_ref, xact_ref, zc_ref, dtT_ref,
                   qkv32_ref, by4_ref, *, tm, tiles_per_seq):
    ti = pl.program_id(0) % tiles_per_seq
    d4, d16 = ATT_CFGS[1][1], ATT_CFGS[2][1]
    x = jnp.concatenate([x_ref[...], xprev_ref[...], xnext_ref[...]], axis=0)
    ms = jnp.mean(x * x, axis=-1, keepdims=True)
    xn_ext = (x * lax.rsqrt(ms + EPS) * nw_ref[...]).astype(BF16)
    xn = xn_ext[0:tm, :]
    cos = cos_ref[...]
    sin = sin_ref[...]
    lane = lax.broadcasted_iota(jnp.int32, cos.shape, 1)
    first_half = (lane % HEAD_DIM) < (HEAD_DIM // 2)
    gw = MXU_COLS
    gslabs = gw // LANES
    qk_cols, qkv_cols = 2 * A_WIDTH, 3 * A_WIDTH
    zab_end, xbc_end = qkv_cols + 2048, qkv_cols + 2048 + C_CONV_CH
    erow = lax.broadcasted_iota(jnp.int32, (EDGE_ROWS, gw), 0)

    groups = list(range(0, MAIN_COLS, gw))
    heavy = [c for c in groups if c < qkv_cols or zab_end <= c < xbc_end]
    plain = [c for c in groups if c not in heavy]
    assert len(heavy) == len(plain)
    for c0 in [c for pair in zip(heavy, plain) for c in pair]:
        if c0 < qkv_cols:
            acc = _dot(xn, w_ref[:, c0:c0 + gw])
            scale = HEAD_DIM ** -0.5 * LOG2E if c0 < A_WIDTH else 1.0
            for j in range(gslabs):
                slab = c0 // LANES + j
                out = acc[:, j * LANES:(j + 1) * LANES]
                if c0 < qk_cols:
                    fwd = pltpu.roll(out, HEAD_DIM // 2, 1)
                    bwd = pltpu.roll(out, LANES - HEAD_DIM // 2, 1)
                    out = out * cos + jnp.where(first_half, bwd, fwd) * sin
                    if scale != 1.0:
                        out = out * scale
                qkv32_ref[slab] = out
                qkv_ref[:, slab * LANES:(slab + 1) * LANES] = out.astype(BF16)
                n4 = tm // d4
                for r in range(d4):
                    piece = qkv32_ref[slab, pl.ds(r, n4, stride=d4), :]
                    by4_ref[slab, r * n4:(r + 1) * n4, :] = piece
                    c = r * qkv_cols + slab * LANES
                    qkv4_ref[:, c:c + LANES] = piece.astype(BF16)
                for r in range(d16):
                    piece = by4_ref[slab, pl.ds((r % d4) * n4 + r // d4, tm // d16, stride=d16 // d4), :]
                    c = r * qkv_cols + slab * LANES
                    qkv16_ref[:, c:c + LANES] = piece.astype(BF16)
        elif c0 < zab_end:
            zab_ref[:, c0 - qkv_cols:c0 - qkv_cols + gw] = _dot(xn, w_ref[:, c0:c0 + gw]).astype(BF16)
        elif c0 < xbc_end:
            cols = slice(c0 - zab_end, c0 - zab_end + gw)
            ext = _dot(xn_ext, w_ref[:, c0:c0 + gw])
            cur = ext[0:tm, :]
            prev_row = jnp.where(ti > 0, ext[tm + 7:tm + 8, :], 0.0)
            next_row = jnp.where(ti < tiles_per_seq - 1, ext[tm + 8:tm + 9, :], 0.0)

            def conv_act(xm1, x0, xp1):
                conv = (cw_ref[0:1, cols] * xm1 + cw_ref[1:2, cols] * x0 + cw_ref[2:3, cols] * xp1
                        + cb_ref[:, cols])
                return _silu(conv).astype(BF16)

            xact_ref[:, cols] = conv_act(pltpu.roll(cur, 1, 0), cur, pltpu.roll(cur, tm - 1, 0))
            e = EDGE_ROWS
            top, bot = cur[0:e, :], cur[tm - e:tm, :]
            first, last = erow == 0, erow == e - 1
            top_m1 = jnp.where(first, prev_row, pltpu.roll(top, 1, 0))
            top_p1 = jnp.where(last, cur[e:e + 1, :], pltpu.roll(top, e - 1, 0))
            xact_ref[0:e, cols] = conv_act(top_m1, top, top_p1)
            bot_m1 = jnp.where(first, cur[tm - e - 1:tm - e, :], pltpu.roll(bot, 1, 0))
            bot_p1 = jnp.where(last, next_row, pltpu.roll(bot, e - 1, 0))
            xact_ref[tm - e:tm, cols] = conv_act(bot_m1, bot, bot_p1)
        else:
            zc_ref[:, c0 - xbc_end:c0 - xbc_end + gw] = _dot(xn, w_ref[:, c0:c0 + gw]).astype(BF16)
    dtT = _dot_nt(wdtT_ref[...], xn)
    for c in range(dtT_ref.shape[0]):
        dtT_ref[c] = dtT[:, c * CHUNK:(c + 1) * CHUNK]


def _in_proj(x2d, seq_len, nw, cos_t, sin_t, w_main, w_dtT, conv_w, conv_b):
    T = x2d.shape[0]
    tm = TM_PROJ
    tiles_per_seq = seq_len // tm
    hb = tm // F32_SUBLANES
    last_hb = T // F32_SUBLANES - 1
    d4, d16 = ATT_CFGS[1][1], ATT_CFGS[2][1]
    const = lambda i: (0, 0)
    row = lambda i: (i, 0)
    return pl.pallas_call(
        functools.partial(_inproj_kernel, tm=tm, tiles_per_seq=tiles_per_seq),
        grid=(T // tm,),
        in_specs=[
            pl.BlockSpec((tm, D_MODEL), row),
            pl.BlockSpec((F32_SUBLANES, D_MODEL), lambda i: (jnp.maximum(i * hb - 1, 0), 0)),
            pl.BlockSpec((F32_SUBLANES, D_MODEL), lambda i: (jnp.minimum((i + 1) * hb, last_hb), 0)),
            pl.BlockSpec((1, D_MODEL), const),
            pl.BlockSpec((tm, LANES), lambda i: (i % tiles_per_seq, 0)),
            pl.BlockSpec((tm, LANES), lambda i: (i % tiles_per_seq, 0)),
            pl.BlockSpec((D_MODEL, MAIN_COLS), const),
            pl.BlockSpec((DT_COLS, D_MODEL), const),
            pl.BlockSpec((3, C_CONV_CH), const),
            pl.BlockSpec((1, C_CONV_CH), const),
        ],
        out_specs=[
            pl.BlockSpec((tm, 3 * A_WIDTH), row),
            pl.BlockSpec((tm // d4, d4 * 3 * A_WIDTH), row),
            pl.BlockSpec((tm // d16, d16 * 3 * A_WIDTH), row),
            pl.BlockSpec((tm, 2048), row),
            pl.BlockSpec((tm, C_CONV_CH), row),
            pl.BlockSpec((tm, C_WIDTH), row),
            pl.BlockSpec((tm // CHUNK, DT_COLS, CHUNK), lambda i: (i, 0, 0)),
        ],
        out_shape=[
            jax.ShapeDtypeStruct((T, 3 * A_WIDTH), BF16),
            jax.ShapeDtypeStruct((T // d4, d4 * 3 * A_WIDTH), BF16),
            jax.ShapeDtypeStruct((T // d16, d16 * 3 * A_WIDTH), BF16),
            jax.ShapeDtypeStruct((T, 2048), BF16),
            jax.ShapeDtypeStruct((T, C_CONV_CH), BF16),
            jax.ShapeDtypeStruct((T, C_WIDTH), BF16),
            jax.ShapeDtypeStruct((T // CHUNK, DT_COLS, CHUNK), F32),
        ],
        scratch_shapes=[pltpu.VMEM((3 * A_WIDTH // LANES, tm, LANES), F32),
                        pltpu.VMEM((3 * A_WIDTH // LANES, tm, LANES), F32)],
        compiler_params=pltpu.CompilerParams(
            dimension_semantics=("arbitrary",), vmem_limit_bytes=VMEM_LIMIT_BYTES),
        name="in_proj",
    )(x2d, x2d, x2d, nw, cos_t, sin_t, w_main, w_dtT, conv_w, conv_b)


def _attn_kernel(q_ref, kp_ref, kc_ref, kn_ref, vp_ref, vc_ref, vn_ref, o_ref, lse_ref,
                 kbuf, vbuf, *, qb, sub_len):
    n = pl.program_id(2)
    kbuf[0:ATT_Q, :] = kp_ref[...]
    kbuf[ATT_Q:ATT_Q + qb, :] = kc_ref[...]
    kbuf[ATT_Q + qb:, :] = kn_ref[...]
    vbuf[0:ATT_Q, :] = vp_ref[...]
    vbuf[ATT_Q:ATT_Q + qb, :] = vc_ref[...]
    vbuf[ATT_Q + qb:, :] = vn_ref[...]

    sq = ATT_SUBQ
    nkeys = MXU_COLS
    assert sq + 2 * ATT_HALF <= nkeys
    qq = lax.broadcasted_iota(jnp.int32, (sq, nkeys), 0)
    kk = lax.broadcasted_iota(jnp.int32, (sq, nkeys), 1)
    kk1 = lax.broadcasted_iota(jnp.int32, (1, nkeys), 1)
    lane = lax.broadcasted_iota(jnp.int32, (sq, LANES), 1)
    low_half = lane < HEAD_DIM
    high_half = jnp.logical_not(low_half)
    band_bias = jnp.where((kk >= qq) & (kk <= qq + 2 * ATT_HALF), 0.0, NEG)
    ones = jnp.ones((nkeys, LANES), BF16)

    for i in range(qb // sq):
        r0 = i * sq
        k0 = i * sq + ATT_Q - ATT_HALF
        q = q_ref[r0:r0 + sq, :]
        kw = kbuf[k0:k0 + nkeys, :]
        vw = vbuf[k0:k0 + nkeys, :]
        pos = kk1 + (n * qb + i * sq - ATT_HALF)
        bias = band_bias + jnp.where((pos >= 0) & (pos < sub_len), 0.0, NEG)
        rows = slice(r0, r0 + sq)
        lse_ref[rows, :] = jnp.zeros((sq, LANES), F32)
        for j in range(A_WIDTH // LANES):
            cols = slice(j * LANES, (j + 1) * LANES)
            qp, kp, vp = q[:, cols], kw[:, cols], vw[:, cols]
            zero = jnp.zeros_like(qp)
            qs = jnp.concatenate([jnp.where(low_half, qp, zero), jnp.where(high_half, qp, zero)], axis=0)
            s = _dot_nt(qs, kp)
            ps = []
            for e in range(2):
                se = s[e * sq:(e + 1) * sq, :] + bias
                m = jnp.max(se, axis=-1, keepdims=True)
                ps.append(jnp.exp2(se - m).astype(BF16))
                lse_ref[rows, 2 * j + e:2 * j + e + 1] = m
            o = _dot(jnp.concatenate(ps, axis=0), jnp.concatenate([vp, ones], axis=1))
            for e in range(2):
                h = A_HEADS + 2 * j + e
                lse_ref[rows, h:h + 1] = o[e * sq:(e + 1) * sq, LANES + h:LANES + h + 1]
            o_ref[rows, cols] = jnp.where(low_half, o[0:sq, 0:LANES], o[sq:, 0:LANES]).astype(BF16)


def _attention(view, dil):
    Bn, L, _ = view.shape
    qb = min(ATT_QB, L)
    nb = L // qb
    r128 = qb // ATT_Q
    last128 = L // ATT_Q - 1

    def cur(c):
        return pl.BlockSpec((None, qb, A_WIDTH), lambda b, r, n: (b, n, 3 * r + c))

    def prev(c):
        return pl.BlockSpec((None, ATT_Q, A_WIDTH),
                            lambda b, r, n: (b, jnp.maximum(n * r128 - 1, 0), 3 * r + c))

    def nxt(c):
        return pl.BlockSpec((None, ATT_Q, A_WIDTH),
                            lambda b, r, n: (b, jnp.minimum((n + 1) * r128, last128), 3 * r + c))

    return pl.pallas_call(
        functools.partial(_attn_kernel, qb=qb, sub_len=L),
        grid=(Bn, dil, nb),
        in_specs=[cur(0), prev(1), cur(1), nxt(1), prev(2), cur(2), nxt(2)],
        out_specs=[
            pl.BlockSpec((None, qb, A_WIDTH), lambda b, r, n: (b, n, r)),
            pl.BlockSpec((None, qb, LANES), lambda b, r, n: (b, n, r)),
        ],
        out_shape=[
            jax.ShapeDtypeStruct((Bn, L, dil * A_WIDTH), BF16),
            jax.ShapeDtypeStruct((Bn, L, dil * LANES), F32),
        ],
        scratch_shapes=[
            pltpu.VMEM((qb + 2 * ATT_Q, A_WIDTH), BF16),
            pltpu.VMEM((qb + 2 * ATT_Q, A_WIDTH), BF16),
        ],
        compiler_params=pltpu.CompilerParams(
            dimension_semantics=("arbitrary", "arbitrary", "arbitrary"),
            vmem_limit_bytes=VMEM_LIMIT_BYTES),
        name=f"attn_d{dil}",
    )(view, view, view, view, view, view, view)


def _ssd_chunk(xs, bm, cm, dt_rawT, bias_col, alog_col, state_ref, reverse, between_pairs=None):
    L = CHUNK
    dtT = _softplus(dt_rawT + bias_col)
    dtAT = dtT * (-jnp.exp(alog_col))
    ri = lax.broadcasted_iota(jnp.int32, (L, L), 0)
    ci = lax.broadcasted_iota(jnp.int32, (L, L), 1)
    tri = (ci >= ri) if reverse else (ci <= ri)
    tri_bf = jnp.where(tri, 1.0, 0.0).astype(BF16)
    acumT = sum(_dot_nt(t, tri_bf) for t in _split3(dtAT))
    e = 0 if reverse else L - 1
    edge_col = acumT[:, e:e + 1]
    wT = jnp.exp(edge_col - acumT) * dtT
    cdec = jnp.exp(edge_col)
    a2T = acumT * LOG2E
    a2 = a2T.T
    srcT = a2T - jnp.log2(dtT)
    lane = lax.broadcasted_iota(jnp.int32, (L, LANES), 1)
    low_half = lane < HEAD_DIM
    high_half = jnp.logical_not(low_half)

    def block_diag(t):
        zero = jnp.zeros_like(t)
        return jnp.concatenate([jnp.where(low_half, t, zero), jnp.where(high_half, t, zero)], axis=0)

    ys = []
    for g in range(C_GROUPS):
        bg = bm[:, g * C_STATE:(g + 1) * C_STATE]
        cg = cm[:, g * C_STATE:(g + 1) * C_STATE]
        cb = _dot_nt(cg, bg).astype(BF16)
        bgT = bg.astype(F32).T.astype(BF16)
        for pr in range(4):
            pair = g * 4 + pr
            cols = slice(pair * LANES, (pair + 1) * LANES)
            xbd = block_diag(xs[:, cols])
            prev = state_ref[:, cols]
            pbd = block_diag(prev.astype(BF16))
            ws, css, bss, cd = [], [], [], []
            for k in range(2):
                h = 2 * pair + k
                col = jnp.broadcast_to(a2[:, h:h + 1], (L, L))
                dec = jnp.exp2(jnp.where(tri, col - srcT[h:h + 1, :], NEG))
                ws.append(cb * dec.astype(BF16))
                css.append(cg * jnp.exp2(col).astype(BF16))
                bss.append(bgT * jnp.broadcast_to(wT[h:h + 1, :], (C_STATE, L)).astype(BF16))
                cd.append(cdec[h:h + 1, :])
            ys.append(_dot(jnp.concatenate(ws + css, axis=1), jnp.concatenate([xbd, pbd], axis=0)))
            decay = jnp.where(low_half[0:1, :], cd[0], cd[1])
            state_ref[:, cols] = prev * decay + _dot(jnp.concatenate(bss, axis=1), xbd)
            if between_pairs is not None:
                between_pairs(pair)
    return ys


def _ssd_fwd_kernel(xact_ref, dtT_ref, bcol_ref, acol_ref, dskip_ref, yf_ref, state_ref, *, ts):
    s = pl.program_id(1)

    @pl.when(s == 0)
    def _():
        state_ref[...] = jnp.zeros_like(state_ref)

    bcol, acol = bcol_ref[0:C_HEADS, :], acol_ref[0:C_HEADS, :]

    def body(c, carry):
        r0 = pl.multiple_of(c * CHUNK, CHUNK)
        rows = pl.ds(r0, CHUNK)
        xs = xact_ref[rows, 0:C_WIDTH]
        bm = xact_ref[rows, C_WIDTH:C_WIDTH + C_GROUPS * C_STATE]
        cm = xact_ref[rows, C_WIDTH + C_GROUPS * C_STATE:C_CONV_CH]
        ys = _ssd_chunk(xs, bm, cm, dtT_ref[c, 0:C_HEADS, :], bcol, acol, state_ref, reverse=False)
        for pair, y in enumerate(ys):
            cols = slice(pair * LANES, (pair + 1) * LANES)
            y = y + dskip_ref[:, cols] * xs[:, cols].astype(F32)
            yf_ref[rows, cols] = y.astype(BF16)
        return carry

    lax.fori_loop(0, ts // CHUNK, body, 0, unroll=2)


def _ssd_fwd(xact3, dtT, bcol, acol, dskip):
    Bn, S, _ = xact3.shape
    ts = TS_SCAN
    ns = S // ts
    const = lambda b, s: (0, 0)
    tile = lambda b, s: (b, s, 0)
    return pl.pallas_call(
        functools.partial(_ssd_fwd_kernel, ts=ts),
        grid=(Bn, ns),
        in_specs=[
            pl.BlockSpec((None, ts, C_CONV_CH), tile),
            pl.BlockSpec((ts // CHUNK, DT_COLS, CHUNK), lambda b, s: (b * ns + s, 0, 0)),
            pl.BlockSpec((DT_COLS, 1), const),
            pl.BlockSpec((DT_COLS, 1), const),
            pl.BlockSpec((1, C_WIDTH), const),
        ],
        out_specs=pl.BlockSpec((None, ts, C_WIDTH), tile),
        out_shape=jax.ShapeDtypeStruct((Bn, S, C_WIDTH), BF16),
        scratch_shapes=[pltpu.VMEM((C_STATE, C_WIDTH), F32)],
        compiler_params=pltpu.CompilerParams(
            dimension_semantics=("arbitrary", "arbitrary"), vmem_limit_bytes=VMEM_LIMIT_BYTES),
        name="ssd_fwd",
    )(xact3, dtT, bcol, acol, dskip)


def _mix_out_kernel(xact_ref, dtT_ref, yf_ref, zc_ref, zab_ref,
                    o1_ref, o4_ref, o16_ref, l1_ref, l4_ref, l16_ref, h_ref,
                    bcol_ref, acol_ref,
                    sguw_ref, sgub_ref, e8_ref, nrm_ref, wout_ref, fnw_ref,
                    out_ref, state_ref, ybuf_ref, mixed_ref, onat_ref, snat_ref, *, ts, final):
    s = pl.program_id(1)

    @pl.when(s == 0)
    def _():
        state_ref[...] = jnp.zeros_like(state_ref)

    slabs = A_WIDTH // LANES
    for ci, (d, o_ref, l_ref) in enumerate(((ATT_CFGS[1][1], o4_ref, l4_ref),
                                            (ATT_CFGS[2][1], o16_ref, l16_ref))):
        for r in range(d):
            dst_rows = pl.ds(r, ts // d, stride=d)
            for j in range(slabs):
                c0 = r * A_WIDTH + j * LANES
                onat_ref[ci, j, dst_rows, :] = o_ref[:, c0:c0 + LANES].astype(F32)
            snat_ref[ci, dst_rows, :] = l_ref[:, r * LANES:(r + 1) * LANES]

    bcol, acol = bcol_ref[C_HEADS:DT_COLS, :], acol_ref[C_HEADS:DT_COLS, :]
    nch = ts // CHUNK

    def out_proj_piece(rows, q):
        cols = slice(q * MXU_COLS, (q + 1) * MXU_COLS)
        out_ref[rows, cols] = h_ref[rows, cols] + _dot(mixed_ref[rows, :], wout_ref[:, cols])

    def out_proj_finish(rows):
        if final:
            hn = out_ref[rows, :]
            ms = jnp.mean(hn * hn, axis=-1, keepdims=True)
            out_ref[rows, :] = hn * lax.rsqrt(ms + EPS) * fnw_ref[...]

    n_pieces = D_MODEL // MXU_COLS
    pairs_per_piece = (C_HEADS // 2) // n_pieces

    def mix_chunk(c, done_rows):
        rows = slice(c * CHUNK, (c + 1) * CHUNK)

        def between_pairs(pair):
            if done_rows is not None and pair % pairs_per_piece == pairs_per_piece - 1:
                out_proj_piece(done_rows, pair // pairs_per_piece)

        xs = xact_ref[rows, 0:C_WIDTH]
        bm = xact_ref[rows, C_WIDTH:C_WIDTH + C_GROUPS * C_STATE]
        cm = xact_ref[rows, C_WIDTH + C_GROUPS * C_STATE:C_CONV_CH]
        ys = _ssd_chunk(xs, bm, cm, dtT_ref[c, C_HEADS:DT_COLS, :], bcol, acol, state_ref, reverse=True,
                        between_pairs=between_pairs)
        if done_rows is not None:
            out_proj_finish(done_rows)
        ssq = jnp.zeros((CHUNK, 1), F32)
        for pair, y in enumerate(ys):
            cols = slice(pair * LANES, (pair + 1) * LANES)
            y = (y + yf_ref[rows, cols].astype(F32)) * _silu(zc_ref[rows, cols].astype(F32))
            ybuf_ref[rows, cols] = y
            ssq = ssq + jnp.sum(y * y, axis=-1, keepdims=True)
        inv = lax.rsqrt(ssq * (1.0 / C_WIDTH) + EPS)
        mixed_ref[rows, A_WIDTH + B_WIDTH:D_MIX] = (ybuf_ref[rows, :] * inv * nrm_ref[...]).astype(BF16)

        st = [l1_ref[rows, :], snat_ref[0, rows, :], snat_ref[1, rows, :]]
        mx = jnp.maximum(jnp.maximum(st[0], st[1]), st[2])
        es = [jnp.exp2(t - mx) for t in st]
        dens = [pltpu.roll(t, LANES - A_HEADS, 1) for t in st]
        z = es[0] * dens[0] + es[1] * dens[1] + es[2] * dens[2]
        head_lane = lax.broadcasted_iota(jnp.int32, (CHUNK, LANES), 1) < A_HEADS
        rz = 1.0 / jnp.where(head_lane, z, 1.0)
        wexp = [_dot((ec * rz).astype(BF16), e8_ref[...]) for ec in es]
        for j in range(A_WIDTH // LANES):
            cols = slice(j * LANES, (j + 1) * LANES)
            oa = (wexp[0][:, cols] * o1_ref[rows, cols].astype(F32)
                  + wexp[1][:, cols] * onat_ref[0, j, rows, :]
                  + wexp[2][:, cols] * onat_ref[1, j, rows, :])
            za = zab_ref[rows, cols].astype(F32)
            mixed_ref[rows, cols] = (oa * _silu(za)).astype(BF16)

        for g in range(B_GROUPS):
            cols = slice(g * LANES, (g + 1) * LANES)
            ub = zab_ref[rows, A_WIDTH + g * LANES:A_WIDTH + (g + 1) * LANES].astype(F32)
            vb = zab_ref[rows, 2 * A_WIDTH + g * LANES:2 * A_WIDTH + (g + 1) * LANES]
            zb = zab_ref[rows, 3 * A_WIDTH + g * LANES:3 * A_WIDTH + (g + 1) * LANES].astype(F32)
            mixed = _dot(sguw_ref[g], vb) + sgub_ref[:, cols]
            mixed_ref[rows, A_WIDTH + g * LANES:A_WIDTH + (g + 1) * LANES] = (
                ub * mixed * _silu(zb)).astype(BF16)

        return rows

    done = None
    for c in reversed(range(nch)):
        done = mix_chunk(c, done)
    for q in range(n_pieces):
        out_proj_piece(done, q)
    out_proj_finish(done)


def _mix_out(xact3, dtT, yf3, zc3, zab3, o1, o4, o16, l1, l4, l16, h3,
             bcol, acol, sguw, sgub, e8, nrm, wout, fnw, final):
    Bn, S, _ = xact3.shape
    ts = TS_SEQ
    ns = S // ts
    const2 = lambda b, s: (0, 0)
    const3 = lambda b, s: (0, 0, 0)
    tile = lambda b, s: (b, ns - 1 - s, 0)
    tspec = lambda w: pl.BlockSpec((None, ts, w), tile)
    gspec = lambda d, w: pl.BlockSpec((None, ts // d, d * w), tile)
    d4, d16 = ATT_CFGS[1][1], ATT_CFGS[2][1]
    return pl.pallas_call(
        functools.partial(_mix_out_kernel, ts=ts, final=final),
        grid=(Bn, ns),
        in_specs=[
            tspec(C_CONV_CH),
            pl.BlockSpec((ts // CHUNK, DT_COLS, CHUNK), lambda b, s: (b * ns + ns - 1 - s, 0, 0)),
            tspec(C_WIDTH), tspec(C_WIDTH), tspec(2048),
            tspec(A_WIDTH), gspec(d4, A_WIDTH), gspec(d16, A_WIDTH),
            tspec(LANES), gspec(d4, LANES), gspec(d16, LANES),
            tspec(D_MODEL),
            pl.BlockSpec((DT_COLS, 1), const2),
            pl.BlockSpec((DT_COLS, 1), const2),
            pl.BlockSpec((B_GROUPS, CHUNK, CHUNK), const3),
            pl.BlockSpec((CHUNK, B_WIDTH), const2),
            pl.BlockSpec((LANES, A_WIDTH), const2),
            pl.BlockSpec((1, C_WIDTH), const2),
            pl.BlockSpec((D_MIX, D_MODEL), const2),
            pl.BlockSpec((1, D_MODEL), const2),
        ],
        out_specs=pl.BlockSpec((None, ts, D_MODEL), tile),
        out_shape=jax.ShapeDtypeStruct((Bn, S, D_MODEL), F32),
        scratch_shapes=[
            pltpu.VMEM((C_STATE, C_WIDTH), F32),
            pltpu.VMEM((ts, C_WIDTH), F32),
            pltpu.VMEM((ts, D_MIX), BF16),
            pltpu.VMEM((2, A_WIDTH // LANES, ts, LANES), F32),
            pltpu.VMEM((2, ts, LANES), F32),
        ],
        compiler_params=pltpu.CompilerParams(
            dimension_semantics=("arbitrary", "arbitrary"), vmem_limit_bytes=VMEM_LIMIT_BYTES),
        name="mix_out_final" if final else "mix_out",
    )(xact3, dtT, yf3, zc3, zab3, o1, o4, o16, l1, l4, l16, h3,
      bcol, acol, sguw, sgub, e8, nrm, wout, fnw)


def _rope_tables(S):
    inv = ROPE_THETA ** (-jnp.arange(0, HEAD_DIM, 2, dtype=F32) / HEAD_DIM)
    ang = jnp.arange(S, dtype=F32)[:, None] * inv[None, :]
    cos, sin = jnp.cos(ang), jnp.sin(ang)
    cos_t = jnp.concatenate([cos, cos, cos, cos], axis=-1)
    sin_t = jnp.concatenate([-sin, sin, -sin, sin], axis=-1)
    return cos_t, sin_t


def _layer_params(l, norm_w, w_in, sgu_w, sgu_b, conv_w, conv_b, dt_bias, a_log, d_skip,
                  ssd_norm_w, w_out):
    w = w_in[l]
    head_of_lane = jnp.arange(A_WIDTH) // HEAD_DIM
    e8 = (jnp.arange(LANES)[:, None] == head_of_lane[None, :]).astype(BF16)
    return dict(
        nw=norm_w[l][None, :],
        w_main=w[:, :MAIN_COLS].astype(BF16),
        w_dtT=w[:, MAIN_COLS:].T.astype(BF16),
        sguw=sgu_w[l].astype(BF16),
        sgub=jnp.repeat(sgu_b[l].T, B_WIDTH // B_GROUPS, axis=1),
        conv_w=conv_w[l],
        conv_b=conv_b[l][None, :],
        bcol=dt_bias[l].reshape(DT_COLS, 1),
        acol=a_log[l].reshape(DT_COLS, 1),
        dskip=jnp.repeat(d_skip[l], HEAD_DIM)[None, :],
        nrm=ssd_norm_w[l][None, :],
        wout=w_out[l].astype(BF16),
        e8=e8,
    )


def _trunk(x, layers, fnw):
    Bn, S, _ = x.shape
    T = Bn * S
    assert S % (ATT_Q * max(d for _, d in ATT_CFGS)) == 0
    assert S % TS_SEQ == 0 and S % TS_SCAN == 0 and S % TM_PROJ == 0
    cos_t, sin_t = _rope_tables(S)
    h = x
    for li, p in enumerate(layers):
        qkv, qkv4, qkv16, zab, xact, zc, dtT = _in_proj(
            h.reshape(T, D_MODEL), S, p["nw"], cos_t, sin_t, p["w_main"], p["w_dtT"],
            p["conv_w"], p["conv_b"])
        att = [_attention(t.reshape(Bn, S // dil, dil * 3 * A_WIDTH), dil)
               for t, (_, dil) in zip((qkv, qkv4, qkv16), ATT_CFGS)]
        xact3 = xact.reshape(Bn, S, C_CONV_CH)
        yf3 = _ssd_fwd(xact3, dtT, p["bcol"], p["acol"], p["dskip"])
        h = _mix_out(xact3, dtT, yf3, zc.reshape(Bn, S, C_WIDTH), zab.reshape(Bn, S, 2048),
                     att[0][0], att[1][0], att[2][0], att[0][1], att[1][1], att[2][1], h,
                     p["bcol"], p["acol"], p["sguw"], p["sgub"], p["e8"],
                     p["nrm"], p["wout"], fnw, final=(li == len(layers) - 1))
    return h


def kernel(x_prompt, x_sample, norm_w, w_in, sgu_w, sgu_b, conv_w, conv_b, dt_bias, a_log, d_skip,
           ssd_norm_w, w_out, final_norm_w):
    depth = w_in.shape[0]
    layers = [_layer_params(l, norm_w, w_in, sgu_w, sgu_b, conv_w, conv_b, dt_bias, a_log, d_skip,
                            ssd_norm_w, w_out) for l in range(depth)]
    fnw = final_norm_w[None, :]
    return (_trunk(x_prompt, layers, fnw), _trunk(x_sample, layers, fnw))
```

```python
import functools

import jax
import jax.numpy as jnp
from jax import lax
from jax.experimental import pallas as pl
from jax.experimental.pallas import tpu as pltpu

F32 = jnp.float32
BF16 = jnp.bfloat16

D_MODEL = 1024
HEAD_DIM = 64
A_WIDTH = 512
A_HEADS = 8
ATT_CFGS = ((128, 1), (512, 4), (2048, 16))
ATT_Q = 128
ATT_HALF = 64
ROPE_THETA = 10000.0
B_WIDTH = 512
B_GROUPS = 4
CHUNK = 128
C_WIDTH = 1024
C_HEADS = 16
C_STATE = 128
C_GROUPS = 2
C_CONV_CH = C_WIDTH + 2 * C_GROUPS * C_STATE
MAIN_COLS = 6144
DT_COLS = 2 * C_HEADS
D_MIX = 2048
EPS = 1e-5
NEG = -1e30
LOG2E = 1.4426950408889634
LN2 = 0.6931471805599453

VMEM_LIMIT_BYTES = 56 * 1024 * 1024
LANES = 128
F32_SUBLANES = 8
EDGE_ROWS = 16
MXU_COLS = 256

TM_PROJ = 512
TS_SEQ = 512
ATT_QB = 1024
ATT_SUBQ = 64


def _silu(x):
    hx = 0.5 * x
    return hx + hx * jnp.tanh(hx)


def _softplus(x):
    return jnp.maximum(x, 0.0) + jnp.log1p(jnp.exp(-jnp.abs(x)))


def _split3(x):
    hi = x.astype(BF16)
    r1 = x - hi.astype(F32)
    mid = r1.astype(BF16)
    lo = (r1 - mid.astype(F32)).astype(BF16)
    return hi, mid, lo


def _dot(a, b):
    return jnp.dot(a, b, preferred_element_type=F32)


def _dot_nt(a, b):
    return lax.dot_general(a, b, (((1,), (1,)), ((), ())), preferred_element_type=F32)


def _inproj_kernel(x_ref, xprev_ref, xnext_ref, nw_ref, cos_ref, sin_ref, w_ref, wdtT_ref,
                   cw_ref, cb_ref, bcol_ref, acol_ref, dskip_ref,
                   qkv_ref, qkv4_ref, qkv16_ref, zab_ref, xact_ref, zc_ref, dtT_ref, yf_ref,
                   qkv32_ref, by4_ref, state_ref, *, tm, tiles_per_seq):
    ti = pl.program_id(0) % tiles_per_seq
    d4, d16 = ATT_CFGS[1][1], ATT_CFGS[2][1]

    @pl.when(ti == 0)
    def _():
        state_ref[...] = jnp.zeros_like(state_ref)

    x = jnp.concatenate([x_ref[...], xprev_ref[...], xnext_ref[...]], axis=0)
    ms = jnp.mean(x * x, axis=-1, keepdims=True)
    xn_ext = (x * lax.rsqrt(ms + EPS) * nw_ref[...]).astype(BF16)
    xn = xn_ext[0:tm, :]
    cos = cos_ref[...]
    sin = sin_ref[...]
    lane = lax.broadcasted_iota(jnp.int32, cos.shape, 1)
    first_half = (lane % HEAD_DIM) < (HEAD_DIM // 2)
    gw = MXU_COLS
    gslabs = gw // LANES
    qk_cols, qkv_cols = 2 * A_WIDTH, 3 * A_WIDTH
    zab_end, xbc_end = qkv_cols + 2048, qkv_cols + 2048 + C_CONV_CH
    erow = lax.broadcasted_iota(jnp.int32, (EDGE_ROWS, gw), 0)

    def emit_group(c0):
        if c0 < qkv_cols:
            acc = _dot(xn, w_ref[:, c0:c0 + gw])
            scale = HEAD_DIM ** -0.5 * LOG2E if c0 < A_WIDTH else 1.0
            for j in range(gslabs):
                slab = c0 // LANES + j
                out = acc[:, j * LANES:(j + 1) * LANES]
                if c0 < qk_cols:
                    fwd = pltpu.roll(out, HEAD_DIM // 2, 1)
                    bwd = pltpu.roll(out, LANES - HEAD_DIM // 2, 1)
                    out = out * cos + jnp.where(first_half, bwd, fwd) * sin
                    if scale != 1.0:
                        out = out * scale
                qkv32_ref[slab] = out
                qkv_ref[:, slab * LANES:(slab + 1) * LANES] = out.astype(BF16)
                n4 = tm // d4
                for r in range(d4):
                    piece = qkv32_ref[slab, pl.ds(r, n4, stride=d4), :]
                    by4_ref[slab, r * n4:(r + 1) * n4, :] = piece
                    c = r * qkv_cols + slab * LANES
                    qkv4_ref[:, c:c + LANES] = piece.astype(BF16)
                for r in range(d16):
                    piece = by4_ref[slab, pl.ds((r % d4) * n4 + r // d4, tm // d16, stride=d16 // d4), :]
                    c = r * qkv_cols + slab * LANES
                    qkv16_ref[:, c:c + LANES] = piece.astype(BF16)
        elif c0 < zab_end:
            zab_ref[:, c0 - qkv_cols:c0 - qkv_cols + gw] = _dot(xn, w_ref[:, c0:c0 + gw]).astype(BF16)
        elif c0 < xbc_end:
            cols = slice(c0 - zab_end, c0 - zab_end + gw)
            ext = _dot(xn_ext, w_ref[:, c0:c0 + gw])
            cur = ext[0:tm, :]
            prev_row = jnp.where(ti > 0, ext[tm + 7:tm + 8, :], 0.0)
            next_row = jnp.where(ti < tiles_per_seq - 1, ext[tm + 8:tm + 9, :], 0.0)

            def conv_act(xm1, x0, xp1):
                conv = (cw_ref[0:1, cols] * xm1 + cw_ref[1:2, cols] * x0 + cw_ref[2:3, cols] * xp1
                        + cb_ref[:, cols])
                return _silu(conv).astype(BF16)

            xact_ref[:, cols] = conv_act(pltpu.roll(cur, 1, 0), cur, pltpu.roll(cur, tm - 1, 0))
            e = EDGE_ROWS
            top, bot = cur[0:e, :], cur[tm - e:tm, :]
            first, last = erow == 0, erow == e - 1
            top_m1 = jnp.where(first, prev_row, pltpu.roll(top, 1, 0))
            top_p1 = jnp.where(last, cur[e:e + 1, :], pltpu.roll(top, e - 1, 0))
            xact_ref[0:e, cols] = conv_act(top_m1, top, top_p1)
            bot_m1 = jnp.where(first, cur[tm - e - 1:tm - e, :], pltpu.roll(bot, 1, 0))
            bot_p1 = jnp.where(last, next_row, pltpu.roll(bot, e - 1, 0))
            xact_ref[tm - e:tm, cols] = conv_act(bot_m1, bot, bot_p1)
        else:
            zc_ref[:, c0 - xbc_end:c0 - xbc_end + gw] = _dot(xn, w_ref[:, c0:c0 + gw]).astype(BF16)

    dtT = _dot_nt(wdtT_ref[...], xn)
    nch = tm // CHUNK
    for c in range(nch):
        dtT_ref[c] = dtT[:, c * CHUNK:(c + 1) * CHUNK]

    groups = list(range(0, MAIN_COLS, gw))
    conv_groups = [c for c in groups if zab_end <= c < xbc_end]
    qkv_groups = [c for c in groups if c < qkv_cols]
    plain = [c for c in groups if c not in conv_groups and c not in qkv_groups]
    for heavy, light in zip(conv_groups, plain):
        emit_group(heavy)
        emit_group(light)
    plain = plain[len(conv_groups):]
    later = [c for pair in zip(qkv_groups, plain) for c in pair]
    assert len(qkv_groups) == len(plain)
    slots = nch * (C_HEADS // 2)
    emitted = [0]

    def between_pairs_of(chunk):
        def hook(pair):
            slot = chunk * (C_HEADS // 2) + pair + 1
            while emitted[0] < len(later) and emitted[0] * slots < slot * len(later):
                emit_group(later[emitted[0]])
                emitted[0] += 1
        return hook

    bcol, acol = bcol_ref[0:C_HEADS, :], acol_ref[0:C_HEADS, :]
    for c in range(nch):
        rows = slice(c * CHUNK, (c + 1) * CHUNK)
        xs = xact_ref[rows, 0:C_WIDTH]
        bm = xact_ref[rows, C_WIDTH:C_WIDTH + C_GROUPS * C_STATE]
        cm = xact_ref[rows, C_WIDTH + C_GROUPS * C_STATE:C_CONV_CH]
        ys = _ssd_chunk(xs, bm, cm, dtT[0:C_HEADS, rows], bcol, acol, state_ref, reverse=False,
                        between_pairs=between_pairs_of(c))
        for pair, y in enumerate(ys):
            cols = slice(pair * LANES, (pair + 1) * LANES)
            y = y + dskip_ref[:, cols] * xs[:, cols].astype(F32)
            yf_ref[rows, cols] = y.astype(BF16)
    assert emitted[0] == len(later)


def _in_proj(x2d, seq_len, nw, cos_t, sin_t, w_main, w_dtT, conv_w, conv_b, bcol, acol, dskip):
    T = x2d.shape[0]
    tm = TM_PROJ
    tiles_per_seq = seq_len // tm
    hb = tm // F32_SUBLANES
    last_hb = T // F32_SUBLANES - 1
    d4, d16 = ATT_CFGS[1][1], ATT_CFGS[2][1]
    const = lambda i: (0, 0)
    row = lambda i: (i, 0)
    return pl.pallas_call(
        functools.partial(_inproj_kernel, tm=tm, tiles_per_seq=tiles_per_seq),
        grid=(T // tm,),
        in_specs=[
            pl.BlockSpec((tm, D_MODEL), row),
            pl.BlockSpec((F32_SUBLANES, D_MODEL), lambda i: (jnp.maximum(i * hb - 1, 0), 0)),
            pl.BlockSpec((F32_SUBLANES, D_MODEL), lambda i: (jnp.minimum((i + 1) * hb, last_hb), 0)),
            pl.BlockSpec((1, D_MODEL), const),
            pl.BlockSpec((tm, LANES), lambda i: (i % tiles_per_seq, 0)),
            pl.BlockSpec((tm, LANES), lambda i: (i % tiles_per_seq, 0)),
            pl.BlockSpec((D_MODEL, MAIN_COLS), const),
            pl.BlockSpec((DT_COLS, D_MODEL), const),
            pl.BlockSpec((3, C_CONV_CH), const),
            pl.BlockSpec((1, C_CONV_CH), const),
            pl.BlockSpec((DT_COLS, 1), const),
            pl.BlockSpec((DT_COLS, 1), const),
            pl.BlockSpec((1, C_WIDTH), const),
        ],
        out_specs=[
            pl.BlockSpec((tm, 3 * A_WIDTH), row),
            pl.BlockSpec((tm // d4, d4 * 3 * A_WIDTH), row),
            pl.BlockSpec((tm // d16, d16 * 3 * A_WIDTH), row),
            pl.BlockSpec((tm, 2048), row),
            pl.BlockSpec((tm, C_CONV_CH), row),
            pl.BlockSpec((tm, C_WIDTH), row),
            pl.BlockSpec((tm // CHUNK, DT_COLS, CHUNK), lambda i: (i, 0, 0)),
            pl.BlockSpec((tm, C_WIDTH), row),
        ],
        out_shape=[
            jax.ShapeDtypeStruct((T, 3 * A_WIDTH), BF16),
            jax.ShapeDtypeStruct((T // d4, d4 * 3 * A_WIDTH), BF16),
            jax.ShapeDtypeStruct((T // d16, d16 * 3 * A_WIDTH), BF16),
            jax.ShapeDtypeStruct((T, 2048), BF16),
            jax.ShapeDtypeStruct((T, C_CONV_CH), BF16),
            jax.ShapeDtypeStruct((T, C_WIDTH), BF16),
            jax.ShapeDtypeStruct((T // CHUNK, DT_COLS, CHUNK), F32),
            jax.ShapeDtypeStruct((T, C_WIDTH), BF16),
        ],
        scratch_shapes=[pltpu.VMEM((3 * A_WIDTH // LANES, tm, LANES), F32),
                        pltpu.VMEM((3 * A_WIDTH // LANES, tm, LANES), F32),
                        pltpu.VMEM((C_STATE, C_WIDTH), F32)],
        compiler_params=pltpu.CompilerParams(
            dimension_semantics=("arbitrary",), vmem_limit_bytes=VMEM_LIMIT_BYTES),
        name="in_proj",
    )(x2d, x2d, x2d, nw, cos_t, sin_t, w_main, w_dtT, conv_w, conv_b, bcol, acol, dskip)


def _attn_kernel(q_ref, kp_ref, kc_ref, kn_ref, vp_ref, vc_ref, vn_ref, o_ref, lse_ref,
                 kbuf, vbuf, *, qb, sub_len):
    n = pl.program_id(2)
    kbuf[0:ATT_Q, :] = kp_ref[...]
    kbuf[ATT_Q:ATT_Q + qb, :] = kc_ref[...]
    kbuf[ATT_Q + qb:, :] = kn_ref[...]
    vbuf[0:ATT_Q, :] = vp_ref[...]
    vbuf[ATT_Q:ATT_Q + qb, :] = vc_ref[...]
    vbuf[ATT_Q + qb:, :] = vn_ref[...]

    sq = ATT_SUBQ
    nkeys = MXU_COLS
    assert sq + 2 * ATT_HALF <= nkeys
    qq = lax.broadcasted_iota(jnp.int32, (sq, nkeys), 0)
    kk = lax.broadcasted_iota(jnp.int32, (sq, nkeys), 1)
    kk1 = lax.broadcasted_iota(jnp.int32, (1, nkeys), 1)
    lane = lax.broadcasted_iota(jnp.int32, (sq, LANES), 1)
    low_half = lane < HEAD_DIM
    high_half = jnp.logical_not(low_half)
    band_bias = jnp.where((kk >= qq) & (kk <= qq + 2 * ATT_HALF), 0.0, NEG)
    ones = jnp.ones((nkeys, LANES), BF16)

    for i in range(qb // sq):
        r0 = i * sq
        k0 = i * sq + ATT_Q - ATT_HALF
        q = q_ref[r0:r0 + sq, :]
        kw = kbuf[k0:k0 + nkeys, :]
        vw = vbuf[k0:k0 + nkeys, :]
        pos = kk1 + (n * qb + i * sq - ATT_HALF)
        bias = band_bias + jnp.where((pos >= 0) & (pos < sub_len), 0.0, NEG)
        rows = slice(r0, r0 + sq)
        lse_ref[rows, :] = jnp.zeros((sq, LANES), F32)
        for j in range(A_WIDTH // LANES):
            cols = slice(j * LANES, (j + 1) * LANES)
            qp, kp, vp = q[:, cols], kw[:, cols], vw[:, cols]
            zero = jnp.zeros_like(qp)
            qs = jnp.concatenate([jnp.where(low_half, qp, zero), jnp.where(high_half, qp, zero)], axis=0)
            s = _dot_nt(qs, kp)
            ps = []
            for e in range(2):
                se = s[e * sq:(e + 1) * sq, :] + bias
                m = jnp.max(se, axis=-1, keepdims=True)
                ps.append(jnp.exp2(se - m).astype(BF16))
                lse_ref[rows, 2 * j + e:2 * j + e + 1] = m
            o = _dot(jnp.concatenate(ps, axis=0), jnp.concatenate([vp, ones], axis=1))
            for e in range(2):
                h = A_HEADS + 2 * j + e
                lse_ref[rows, h:h + 1] = o[e * sq:(e + 1) * sq, LANES + h:LANES + h + 1]
            o_ref[rows, cols] = jnp.where(low_half, o[0:sq, 0:LANES], o[sq:, 0:LANES]).astype(BF16)


def _attention(view, dil):
    Bn, L, _ = view.shape
    qb = min(ATT_QB, L)
    nb = L // qb
    r128 = qb // ATT_Q
    last128 = L // ATT_Q - 1

    def cur(c):
        return pl.BlockSpec((None, qb, A_WIDTH), lambda b, r, n: (b, n, 3 * r + c))

    def prev(c):
        return pl.BlockSpec((None, ATT_Q, A_WIDTH),
                            lambda b, r, n: (b, jnp.maximum(n * r128 - 1, 0), 3 * r + c))

    def nxt(c):
        return pl.BlockSpec((None, ATT_Q, A_WIDTH),
                            lambda b, r, n: (b, jnp.minimum((n + 1) * r128, last128), 3 * r + c))

    return pl.pallas_call(
        functools.partial(_attn_kernel, qb=qb, sub_len=L),
        grid=(Bn, dil, nb),
        in_specs=[cur(0), prev(1), cur(1), nxt(1), prev(2), cur(2), nxt(2)],
        out_specs=[
            pl.BlockSpec((None, qb, A_WIDTH), lambda b, r, n: (b, n, r)),
            pl.BlockSpec((None, qb, LANES), lambda b, r, n: (b, n, r)),
        ],
        out_shape=[
            jax.ShapeDtypeStruct((Bn, L, dil * A_WIDTH), BF16),
            jax.ShapeDtypeStruct((Bn, L, dil * LANES), F32),
        ],
        scratch_shapes=[
            pltpu.VMEM((qb + 2 * ATT_Q, A_WIDTH), BF16),
            pltpu.VMEM((qb + 2 * ATT_Q, A_WIDTH), BF16),
        ],
        compiler_params=pltpu.CompilerParams(
            dimension_semantics=("arbitrary", "arbitrary", "arbitrary"),
            vmem_limit_bytes=VMEM_LIMIT_BYTES),
        name=f"attn_d{dil}",
    )(view, view, view, view, view, view, view)


def _ssd_chunk(xs, bm, cm, dt_rawT, bias_col, alog_col, state_ref, reverse, between_pairs=None):
    L = CHUNK
    dtT = _softplus(dt_rawT + bias_col)
    dtAT = dtT * (-jnp.exp(alog_col))
    ri = lax.broadcasted_iota(jnp.int32, (L, L), 0)
    ci = lax.broadcasted_iota(jnp.int32, (L, L), 1)
    tri = (ci >= ri) if reverse else (ci <= ri)
    tri_bf = jnp.where(tri, 1.0, 0.0).astype(BF16)
    acumT = sum(_dot_nt(t, tri_bf) for t in _split3(dtAT))
    e = 0 if reverse else L - 1
    edge_col = acumT[:, e:e + 1]
    wT = jnp.exp(edge_col - acumT) * dtT
    cdec = jnp.exp(edge_col)
    a2T = acumT * LOG2E
    a2 = a2T.T
    srcT = a2T - jnp.log2(dtT)
    lane = lax.broadcasted_iota(jnp.int32, (L, LANES), 1)
    low_half = lane < HEAD_DIM
    high_half = jnp.logical_not(low_half)

    def block_diag(t):
        zero = jnp.zeros_like(t)
        return jnp.concatenate([jnp.where(low_half, t, zero), jnp.where(high_half, t, zero)], axis=0)

    ys = []
    for g in range(C_GROUPS):
        bg = bm[:, g * C_STATE:(g + 1) * C_STATE]
        cg = cm[:, g * C_STATE:(g + 1) * C_STATE]
        cb = _dot_nt(cg, bg).astype(BF16)
        bgT = bg.astype(F32).T.astype(BF16)
        for pr in range(4):
            pair = g * 4 + pr
            cols = slice(pair * LANES, (pair + 1) * LANES)
            xbd = block_diag(xs[:, cols])
            prev = state_ref[:, cols]
            pbd = block_diag(prev.astype(BF16))
            ws, css, bss, cd = [], [], [], []
            for k in range(2):
                h = 2 * pair + k
                col = jnp.broadcast_to(a2[:, h:h + 1], (L, L))
                dec = jnp.exp2(jnp.where(tri, col - srcT[h:h + 1, :], NEG))
                ws.append(cb * dec.astype(BF16))
                css.append(cg * jnp.exp2(col).astype(BF16))
                bss.append(bgT * jnp.broadcast_to(wT[h:h + 1, :], (C_STATE, L)).astype(BF16))
                cd.append(cdec[h:h + 1, :])
            ys.append(_dot(jnp.concatenate(ws + css, axis=1), jnp.concatenate([xbd, pbd], axis=0)))
            decay = jnp.where(low_half[0:1, :], cd[0], cd[1])
            state_ref[:, cols] = prev * decay + _dot(jnp.concatenate(bss, axis=1), xbd)
            if between_pairs is not None:
                between_pairs(pair)
    return ys


def _mix_out_kernel(xact_ref, dtT_ref, yf_ref, zc_ref, zab_ref,
                    o1_ref, o4_ref, o16_ref, l1_ref, l4_ref, l16_ref, h_ref,
                    bcol_ref, acol_ref,
                    sguw_ref, sgub_ref, e8_ref, nrm_ref, wout_ref, fnw_ref,
                    out_ref, state_ref, ybuf_ref, mixed_ref, onat_ref, snat_ref, *, ts, final):
    s = pl.program_id(1)

    @pl.when(s == 0)
    def _():
        state_ref[...] = jnp.zeros_like(state_ref)

    slabs = A_WIDTH // LANES
    for ci, (d, o_ref, l_ref) in enumerate(((ATT_CFGS[1][1], o4_ref, l4_ref),
                                            (ATT_CFGS[2][1], o16_ref, l16_ref))):
        for r in range(d):
            dst_rows = pl.ds(r, ts // d, stride=d)
            for j in range(slabs):
                c0 = r * A_WIDTH + j * LANES
                onat_ref[ci, j, dst_rows, :] = o_ref[:, c0:c0 + LANES].astype(F32)
            snat_ref[ci, dst_rows, :] = l_ref[:, r * LANES:(r + 1) * LANES]

    bcol, acol = bcol_ref[C_HEADS:DT_COLS, :], acol_ref[C_HEADS:DT_COLS, :]
    nch = ts // CHUNK

    def out_proj_piece(rows, q):
        cols = slice(q * MXU_COLS, (q + 1) * MXU_COLS)
        out_ref[rows, cols] = h_ref[rows, cols] + _dot(mixed_ref[rows, :], wout_ref[:, cols])

    def out_proj_finish(rows):
        if final:
            hn = out_ref[rows, :]
            ms = jnp.mean(hn * hn, axis=-1, keepdims=True)
            out_ref[rows, :] = hn * lax.rsqrt(ms + EPS) * fnw_ref[...]

    n_pieces = D_MODEL // MXU_COLS
    pairs_per_piece = (C_HEADS // 2) // n_pieces

    def mix_chunk(c, done_rows):
        rows = slice(c * CHUNK, (c + 1) * CHUNK)

        def between_pairs(pair):
            if done_rows is not None and pair % pairs_per_piece == pairs_per_piece - 1:
                out_proj_piece(done_rows, pair // pairs_per_piece)

        xs = xact_ref[rows, 0:C_WIDTH]
        bm = xact_ref[rows, C_WIDTH:C_WIDTH + C_GROUPS * C_STATE]
        cm = xact_ref[rows, C_WIDTH + C_GROUPS * C_STATE:C_CONV_CH]
        ys = _ssd_chunk(xs, bm, cm, dtT_ref[c, C_HEADS:DT_COLS, :], bcol, acol, state_ref, reverse=True,
                        between_pairs=between_pairs)
        if done_rows is not None:
            out_proj_finish(done_rows)
        ssq = jnp.zeros((CHUNK, 1), F32)
        for pair, y in enumerate(ys):
            cols = slice(pair * LANES, (pair + 1) * LANES)
            y = (y + yf_ref[rows, cols].astype(F32)) * _silu(zc_ref[rows, cols].astype(F32))
            ybuf_ref[rows, cols] = y
            ssq = ssq + jnp.sum(y * y, axis=-1, keepdims=True)
        inv = lax.rsqrt(ssq * (1.0 / C_WIDTH) + EPS)
        mixed_ref[rows, A_WIDTH + B_WIDTH:D_MIX] = (ybuf_ref[rows, :] * inv * nrm_ref[...]).astype(BF16)

        st = [l1_ref[rows, :], snat_ref[0, rows, :], snat_ref[1, rows, :]]
        mx = jnp.maximum(jnp.maximum(st[0], st[1]), st[2])
        es = [jnp.exp2(t - mx) for t in st]
        dens = [pltpu.roll(t, LANES - A_HEADS, 1) for t in st]
        z = es[0] * dens[0] + es[1] * dens[1] + es[2] * dens[2]
        head_lane = lax.broadcasted_iota(jnp.int32, (CHUNK, LANES), 1) < A_HEADS
        rz = 1.0 / jnp.where(head_lane, z, 1.0)
        wexp = [_dot((ec * rz).astype(BF16), e8_ref[...]) for ec in es]
        for j in range(A_WIDTH // LANES):
            cols = slice(j * LANES, (j + 1) * LANES)
            oa = (wexp[0][:, cols] * o1_ref[rows, cols].astype(F32)
                  + wexp[1][:, cols] * onat_ref[0, j, rows, :]
                  + wexp[2][:, cols] * onat_ref[1, j, rows, :])
            za = zab_ref[rows, cols].astype(F32)
            mixed_ref[rows, cols] = (oa * _silu(za)).astype(BF16)

        for g in range(B_GROUPS):
            cols = slice(g * LANES, (g + 1) * LANES)
            ub = zab_ref[rows, A_WIDTH + g * LANES:A_WIDTH + (g + 1) * LANES].astype(F32)
            vb = zab_ref[rows, 2 * A_WIDTH + g * LANES:2 * A_WIDTH + (g + 1) * LANES]
            zb = zab_ref[rows, 3 * A_WIDTH + g * LANES:3 * A_WIDTH + (g + 1) * LANES].astype(F32)
            mixed = _dot(sguw_ref[g], vb) + sgub_ref[:, cols]
            mixed_ref[rows, A_WIDTH + g * LANES:A_WIDTH + (g + 1) * LANES] = (
                ub * mixed * _silu(zb)).astype(BF16)

        return rows

    done = None
    for c in reversed(range(nch)):
        done = mix_chunk(c, done)
    for q in range(n_pieces):
        out_proj_piece(done, q)
    out_proj_finish(done)


def _mix_out(xact3, dtT, yf3, zc3, zab3, o1, o4, o16, l1, l4, l16, h3,
             bcol, acol, sguw, sgub, e8, nrm, wout, fnw, final):
    Bn, S, _ = xact3.shape
    ts = TS_SEQ
    ns = S // ts
    const2 = lambda b, s: (0, 0)
    const3 = lambda b, s: (0, 0, 0)
    tile = lambda b, s: (b, ns - 1 - s, 0)
    tspec = lambda w: pl.BlockSpec((None, ts, w), tile)
    gspec = lambda d, w: pl.BlockSpec((None, ts // d, d * w), tile)
    d4, d16 = ATT_CFGS[1][1], ATT_CFGS[2][1]
    return pl.pallas_call(
        functools.partial(_mix_out_kernel, ts=ts, final=final),
        grid=(Bn, ns),
        in_specs=[
            tspec(C_CONV_CH),
            pl.BlockSpec((ts // CHUNK, DT_COLS, CHUNK), lambda b, s: (b * ns + ns - 1 - s, 0, 0)),
            tspec(C_WIDTH), tspec(C_WIDTH), tspec(2048),
            tspec(A_WIDTH), gspec(d4, A_WIDTH), gspec(d16, A_WIDTH),
            tspec(LANES), gspec(d4, LANES), gspec(d16, LANES),
            tspec(D_MODEL),
            pl.BlockSpec((DT_COLS, 1), const2),
            pl.BlockSpec((DT_COLS, 1), const2),
            pl.BlockSpec((B_GROUPS, CHUNK, CHUNK), const3),
            pl.BlockSpec((CHUNK, B_WIDTH), const2),
            pl.BlockSpec((LANES, A_WIDTH), const2),
            pl.BlockSpec((1, C_WIDTH), const2),
            pl.BlockSpec((D_MIX, D_MODEL), const2),
            pl.BlockSpec((1, D_MODEL), const2),
        ],
        out_specs=pl.BlockSpec((None, ts, D_MODEL), tile),
        out_shape=jax.ShapeDtypeStruct((Bn, S, D_MODEL), F32),
        scratch_shapes=[
            pltpu.VMEM((C_STATE, C_WIDTH), F32),
            pltpu.VMEM((ts, C_WIDTH), F32),
            pltpu.VMEM((ts, D_MIX), BF16),
            pltpu.VMEM((2, A_WIDTH // LANES, ts, LANES), F32),
            pltpu.VMEM((2, ts, LANES), F32),
        ],
        compiler_params=pltpu.CompilerParams(
            dimension_semantics=("arbitrary", "arbitrary"), vmem_limit_bytes=VMEM_LIMIT_BYTES),
        name="mix_out_final" if final else "mix_out",
    )(xact3, dtT, yf3, zc3, zab3, o1, o4, o16, l1, l4, l16, h3,
      bcol, acol, sguw, sgub, e8, nrm, wout, fnw)


def _rope_tables(S):
    inv = ROPE_THETA ** (-jnp.arange(0, HEAD_DIM, 2, dtype=F32) / HEAD_DIM)
    ang = jnp.arange(S, dtype=F32)[:, None] * inv[None, :]
    cos, sin = jnp.cos(ang), jnp.sin(ang)
    cos_t = jnp.concatenate([cos, cos, cos, cos], axis=-1)
    sin_t = jnp.concatenate([-sin, sin, -sin, sin], axis=-1)
    return cos_t, sin_t


def _layer_params(l, norm_w, w_in, sgu_w, sgu_b, conv_w, conv_b, dt_bias, a_log, d_skip,
                  ssd_norm_w, w_out):
    w = w_in[l]
    head_of_lane = jnp.arange(A_WIDTH) // HEAD_DIM
    e8 = (jnp.arange(LANES)[:, None] == head_of_lane[None, :]).astype(BF16)
    return dict(
        nw=norm_w[l][None, :],
        w_main=w[:, :MAIN_COLS].astype(BF16),
        w_dtT=w[:, MAIN_COLS:].T.astype(BF16),
        sguw=sgu_w[l].astype(BF16),
        sgub=jnp.repeat(sgu_b[l].T, B_WIDTH // B_GROUPS, axis=1),
        conv_w=conv_w[l],
        conv_b=conv_b[l][None, :],
        bcol=dt_bias[l].reshape(DT_COLS, 1),
        acol=a_log[l].reshape(DT_COLS, 1),
        dskip=jnp.repeat(d_skip[l], HEAD_DIM)[None, :],
        nrm=ssd_norm_w[l][None, :],
        wout=w_out[l].astype(BF16),
        e8=e8,
    )


def _trunk(x, layers, fnw):
    Bn, S, _ = x.shape
    T = Bn * S
    assert S % (ATT_Q * max(d for _, d in ATT_CFGS)) == 0
    assert S % TS_SEQ == 0 and S % TM_PROJ == 0
    cos_t, sin_t = _rope_tables(S)
    h = x
    for li, p in enumerate(layers):
        qkv, qkv4, qkv16, zab, xact, zc, dtT, yf = _in_proj(
            h.reshape(T, D_MODEL), S, p["nw"], cos_t, sin_t, p["w_main"], p["w_dtT"],
            p["conv_w"], p["conv_b"], p["bcol"], p["acol"], p["dskip"])
        att = [_attention(t.reshape(Bn, S // dil, dil * 3 * A_WIDTH), dil)
               for t, (_, dil) in zip((qkv, qkv4, qkv16), ATT_CFGS)]
        xact3 = xact.reshape(Bn, S, C_CONV_CH)
        h = _mix_out(xact3, dtT, yf.reshape(Bn, S, C_WIDTH), zc.reshape(Bn, S, C_WIDTH),
                     zab.reshape(Bn, S, 2048),
                     att[0][0], att[1][0], att[2][0], att[0][1], att[1][1], att[2][1], h,
                     p["bcol"], p["acol"], p["sguw"], p["sgub"], p["e8"],
                     p["nrm"], p["wout"], fnw, final=(li == len(layers) - 1))
    return h


def kernel(x_prompt, x_sample, norm_w, w_in, sgu_w, sgu_b, conv_w, conv_b, dt_bias, a_log, d_skip,
           ssd_norm_w, w_out, final_norm_w):
    depth = w_in.shape[0]
    layers = [_layer_params(l, norm_w, w_in, sgu_w, sgu_b, conv_w, conv_b, dt_bias, a_log, d_skip,
                            ssd_norm_w, w_out) for l in range(depth)]
    fnw = final_norm_w[None, :]
    return (_trunk(x_prompt, layers, fnw), _trunk(x_sample, layers, fnw))
```

```python
import functools

import jax
import jax.numpy as jnp
from jax import lax
from jax.experimental import pallas as pl
from jax.experimental.pallas import tpu as pltpu

F32 = jnp.float32
BF16 = jnp.bfloat16

D_MODEL = 1024
HEAD_DIM = 64
A_WIDTH = 512
A_HEADS = 8
ATT_CFGS = ((128, 1), (512, 4), (2048, 16))
ATT_Q = 128
ATT_HALF = 64
ROPE_THETA = 10000.0
B_WIDTH = 512
B_GROUPS = 4
CHUNK = 128
C_WIDTH = 1024
C_HEADS = 16
C_STATE = 128
C_GROUPS = 2
C_CONV_CH = C_WIDTH + 2 * C_GROUPS * C_STATE
MAIN_COLS = 6144
DT_COLS = 2 * C_HEADS
D_MIX = 2048
EPS = 1e-5
NEG = -1e30
LOG2E = 1.4426950408889634
LN2 = 0.6931471805599453

VMEM_LIMIT_BYTES = 56 * 1024 * 1024
LANES = 128
F32_SUBLANES = 8
EDGE_ROWS = 16
MXU_COLS = 256

TM_PROJ = 512
TS_SEQ = 512
ATT_QB = 1024
ATT_SUBQ = 64


def _silu(x):
    hx = 0.5 * x
    return hx + hx * jnp.tanh(hx)


def _softplus(x):
    return jnp.maximum(x, 0.0) + jnp.log1p(jnp.exp(-jnp.abs(x)))


def _split3(x):
    hi = x.astype(BF16)
    r1 = x - hi.astype(F32)
    mid = r1.astype(BF16)
    lo = (r1 - mid.astype(F32)).astype(BF16)
    return hi, mid, lo


def _dot(a, b):
    return jnp.dot(a, b, preferred_element_type=F32)


def _dot_nt(a, b):
    return lax.dot_general(a, b, (((1,), (1,)), ((), ())), preferred_element_type=F32)


def _inproj_kernel(x_ref, xprev_ref, xnext_ref, nw_ref, cos_ref, sin_ref, w_ref, wdtT_ref,
                   cw_ref, cb_ref, bcol_ref, acol_ref, dskip_ref,
                   qkv_ref, qkv4_ref, qkv16_ref, zab_ref, xact_ref, zc_ref, dtT_ref, yf_ref,
                   qkv32_ref, by4_ref, state_ref, *, tm, tiles_per_seq):
    ti = pl.program_id(0) % tiles_per_seq
    d4, d16 = ATT_CFGS[1][1], ATT_CFGS[2][1]

    @pl.when(ti == 0)
    def _():
        state_ref[...] = jnp.zeros_like(state_ref)

    x = jnp.concatenate([x_ref[...], xprev_ref[...], xnext_ref[...]], axis=0)
    ms = jnp.mean(x * x, axis=-1, keepdims=True)
    xn_ext = (x * lax.rsqrt(ms + EPS) * nw_ref[...]).astype(BF16)
    xn = xn_ext[0:tm, :]
    cos = cos_ref[...]
    sin = sin_ref[...]
    lane = lax.broadcasted_iota(jnp.int32, cos.shape, 1)
    first_half = (lane % HEAD_DIM) < (HEAD_DIM // 2)
    gw = MXU_COLS
    gslabs = gw // LANES
    qk_cols, qkv_cols = 2 * A_WIDTH, 3 * A_WIDTH
    zab_end, xbc_end = qkv_cols + 2048, qkv_cols + 2048 + C_CONV_CH
    erow = lax.broadcasted_iota(jnp.int32, (EDGE_ROWS, gw), 0)

    def emit_group(c0):
        if c0 < qkv_cols:
            acc = _dot(xn, w_ref[:, c0:c0 + gw])
            scale = HEAD_DIM ** -0.5 * LOG2E if c0 < A_WIDTH else 1.0
            for j in range(gslabs):
                slab = c0 // LANES + j
                out = acc[:, j * LANES:(j + 1) * LANES]
                if c0 < qk_cols:
                    fwd = pltpu.roll(out, HEAD_DIM // 2, 1)
                    bwd = pltpu.roll(out, LANES - HEAD_DIM // 2, 1)
                    out = out * cos + jnp.where(first_half, bwd, fwd) * sin
                    if scale != 1.0:
                        out = out * scale
                qkv32_ref[slab] = out
                qkv_ref[:, slab * LANES:(slab + 1) * LANES] = out.astype(BF16)
                n4 = tm // d4
                for r in range(d4):
                    piece = qkv32_ref[slab, pl.ds(r, n4, stride=d4), :]
                    by4_ref[slab, r * n4:(r + 1) * n4, :] = piece
                    c = r * qkv_cols + slab * LANES
                    qkv4_ref[:, c:c + LANES] = piece.astype(BF16)
                for r in range(d16):
                    piece = by4_ref[slab, pl.ds((r % d4) * n4 + r // d4, tm // d16, stride=d16 // d4), :]
                    c = r * qkv_cols + slab * LANES
                    qkv16_ref[:, c:c + LANES] = piece.astype(BF16)
        elif c0 < zab_end:
            zab_ref[:, c0 - qkv_cols:c0 - qkv_cols + gw] = _dot(xn, w_ref[:, c0:c0 + gw]).astype(BF16)
        elif c0 < xbc_end:
            cols = slice(c0 - zab_end, c0 - zab_end + gw)
            ext = _dot(xn_ext, w_ref[:, c0:c0 + gw])
            cur = ext[0:tm, :]
            prev_row = jnp.where(ti > 0, ext[tm + 7:tm + 8, :], 0.0)
            next_row = jnp.where(ti < tiles_per_seq - 1, ext[tm + 8:tm + 9, :], 0.0)

            def conv_act(xm1, x0, xp1):
                conv = (cw_ref[0:1, cols] * xm1 + cw_ref[1:2, cols] * x0 + cw_ref[2:3, cols] * xp1
                        + cb_ref[:, cols])
                return _silu(conv).astype(BF16)

            xact_ref[:, cols] = conv_act(pltpu.roll(cur, 1, 0), cur, pltpu.roll(cur, tm - 1, 0))
            e = EDGE_ROWS
            top, bot = cur[0:e, :], cur[tm - e:tm, :]
            first, last = erow == 0, erow == e - 1
            top_m1 = jnp.where(first, prev_row, pltpu.roll(top, 1, 0))
            top_p1 = jnp.where(last, cur[e:e + 1, :], pltpu.roll(top, e - 1, 0))
            xact_ref[0:e, cols] = conv_act(top_m1, top, top_p1)
            bot_m1 = jnp.where(first, cur[tm - e - 1:tm - e, :], pltpu.roll(bot, 1, 0))
            bot_p1 = jnp.where(last, next_row, pltpu.roll(bot, e - 1, 0))
            xact_ref[tm - e:tm, cols] = conv_act(bot_m1, bot, bot_p1)
        else:
            zc_ref[:, c0 - xbc_end:c0 - xbc_end + gw] = _dot(xn, w_ref[:, c0:c0 + gw]).astype(BF16)

    dtT = _dot_nt(wdtT_ref[...], xn)
    nch = tm // CHUNK
    for c in range(nch):
        dtT_ref[c] = dtT[:, c * CHUNK:(c + 1) * CHUNK]

    groups = list(range(0, MAIN_COLS, gw))
    conv_groups = [c for c in groups if zab_end <= c < xbc_end]
    qkv_groups = [c for c in groups if c < qkv_cols]
    plain = [c for c in groups if c not in conv_groups and c not in qkv_groups]
    for heavy, light in zip(conv_groups, plain):
        emit_group(heavy)
        emit_group(light)
    plain = plain[len(conv_groups):]
    later = [c for pair in zip(qkv_groups, plain) for c in pair]
    assert len(qkv_groups) == len(plain)
    slots = nch * (C_HEADS // 2)
    emitted = [0]

    def between_pairs_of(chunk):
        def hook(pair):
            slot = chunk * (C_HEADS // 2) + pair + 1
            while emitted[0] < len(later) and emitted[0] * slots < slot * len(later):
                emit_group(later[emitted[0]])
                emitted[0] += 1
        return hook

    bcol, acol = bcol_ref[0:C_HEADS, :], acol_ref[0:C_HEADS, :]
    for c in range(nch):
        rows = slice(c * CHUNK, (c + 1) * CHUNK)
        xs = xact_ref[rows, 0:C_WIDTH]
        bm = xact_ref[rows, C_WIDTH:C_WIDTH + C_GROUPS * C_STATE]
        cm = xact_ref[rows, C_WIDTH + C_GROUPS * C_STATE:C_CONV_CH]
        ys = _ssd_chunk(xs, bm, cm, dtT[0:C_HEADS, rows], bcol, acol, state_ref, reverse=False,
                        between_pairs=between_pairs_of(c))
        for pair, y in enumerate(ys):
            cols = slice(pair * LANES, (pair + 1) * LANES)
            y = y + dskip_ref[:, cols] * xs[:, cols].astype(F32)
            yf_ref[rows, cols] = y.astype(BF16)
    assert emitted[0] == len(later)


def _in_proj(x2d, seq_len, nw, cos_t, sin_t, w_main, w_dtT, conv_w, conv_b, bcol, acol, dskip):
    T = x2d.shape[0]
    tm = TM_PROJ
    tiles_per_seq = seq_len // tm
    hb = tm // F32_SUBLANES
    last_hb = T // F32_SUBLANES - 1
    d4, d16 = ATT_CFGS[1][1], ATT_CFGS[2][1]
    const = lambda i: (0, 0)
    row = lambda i: (i, 0)
    return pl.pallas_call(
        functools.partial(_inproj_kernel, tm=tm, tiles_per_seq=tiles_per_seq),
        grid=(T // tm,),
        in_specs=[
            pl.BlockSpec((tm, D_MODEL), row),
            pl.BlockSpec((F32_SUBLANES, D_MODEL), lambda i: (jnp.maximum(i * hb - 1, 0), 0)),
            pl.BlockSpec((F32_SUBLANES, D_MODEL), lambda i: (jnp.minimum((i + 1) * hb, last_hb), 0)),
            pl.BlockSpec((1, D_MODEL), const),
            pl.BlockSpec((tm, LANES), lambda i: (i % tiles_per_seq, 0)),
            pl.BlockSpec((tm, LANES), lambda i: (i % tiles_per_seq, 0)),
            pl.BlockSpec((D_MODEL, MAIN_COLS), const),
            pl.BlockSpec((DT_COLS, D_MODEL), const),
            pl.BlockSpec((3, C_CONV_CH), const),
            pl.BlockSpec((1, C_CONV_CH), const),
            pl.BlockSpec((DT_COLS, 1), const),
            pl.BlockSpec((DT_COLS, 1), const),
            pl.BlockSpec((1, C_WIDTH), const),
        ],
        out_specs=[
            pl.BlockSpec((tm, 3 * A_WIDTH), row),
            pl.BlockSpec((tm // d4, d4 * 3 * A_WIDTH), row),
            pl.BlockSpec((tm // d16, d16 * 3 * A_WIDTH), row),
            pl.BlockSpec((tm, 2048), row),
            pl.BlockSpec((tm, C_CONV_CH), row),
            pl.BlockSpec((tm, C_WIDTH), row),
            pl.BlockSpec((tm // CHUNK, DT_COLS, CHUNK), lambda i: (i, 0, 0)),
            pl.BlockSpec((tm, C_WIDTH), row),
        ],
        out_shape=[
            jax.ShapeDtypeStruct((T, 3 * A_WIDTH), BF16),
            jax.ShapeDtypeStruct((T // d4, d4 * 3 * A_WIDTH), BF16),
            jax.ShapeDtypeStruct((T // d16, d16 * 3 * A_WIDTH), BF16),
            jax.ShapeDtypeStruct((T, 2048), BF16),
            jax.ShapeDtypeStruct((T, C_CONV_CH), BF16),
            jax.ShapeDtypeStruct((T, C_WIDTH), BF16),
            jax.ShapeDtypeStruct((T // CHUNK, DT_COLS, CHUNK), F32),
            jax.ShapeDtypeStruct((T, C_WIDTH), BF16),
        ],
        scratch_shapes=[pltpu.VMEM((3 * A_WIDTH // LANES, tm, LANES), F32),
                        pltpu.VMEM((3 * A_WIDTH // LANES, tm, LANES), F32),
                        pltpu.VMEM((C_STATE, C_WIDTH), F32)],
        compiler_params=pltpu.CompilerParams(
            dimension_semantics=("arbitrary",), vmem_limit_bytes=VMEM_LIMIT_BYTES),
        name="in_proj",
    )(x2d, x2d, x2d, nw, cos_t, sin_t, w_main, w_dtT, conv_w, conv_b, bcol, acol, dskip)


def _attn_kernel(cur_ref, prev_ref, next_ref, o_ref, lse_ref, kbuf, vbuf, *, qb, sub_len):
    n = pl.program_id(2)
    kcols, vcols = slice(A_WIDTH, 2 * A_WIDTH), slice(2 * A_WIDTH, 3 * A_WIDTH)
    kbuf[0:ATT_Q, :] = prev_ref[:, kcols]
    kbuf[ATT_Q:ATT_Q + qb, :] = cur_ref[:, kcols]
    kbuf[ATT_Q + qb:, :] = next_ref[:, kcols]
    vbuf[0:ATT_Q, :] = prev_ref[:, vcols]
    vbuf[ATT_Q:ATT_Q + qb, :] = cur_ref[:, vcols]
    vbuf[ATT_Q + qb:, :] = next_ref[:, vcols]

    sq = ATT_SUBQ
    nkeys = MXU_COLS
    assert sq + 2 * ATT_HALF <= nkeys
    qq = lax.broadcasted_iota(jnp.int32, (sq, nkeys), 0)
    kk = lax.broadcasted_iota(jnp.int32, (sq, nkeys), 1)
    kk1 = lax.broadcasted_iota(jnp.int32, (1, nkeys), 1)
    lane = lax.broadcasted_iota(jnp.int32, (sq, LANES), 1)
    low_half = lane < HEAD_DIM
    high_half = jnp.logical_not(low_half)
    band_bias = jnp.where((kk >= qq) & (kk <= qq + 2 * ATT_HALF), 0.0, NEG)
    ones = jnp.ones((nkeys, LANES), BF16)

    for i in range(qb // sq):
        r0 = i * sq
        k0 = i * sq + ATT_Q - ATT_HALF
        q = cur_ref[r0:r0 + sq, 0:A_WIDTH]
        kw = kbuf[k0:k0 + nkeys, :]
        vw = vbuf[k0:k0 + nkeys, :]
        pos = kk1 + (n * qb + i * sq - ATT_HALF)
        bias = band_bias + jnp.where((pos >= 0) & (pos < sub_len), 0.0, NEG)
        rows = slice(r0, r0 + sq)
        lse_ref[rows, :] = jnp.zeros((sq, LANES), F32)
        for j in range(A_WIDTH // LANES):
            cols = slice(j * LANES, (j + 1) * LANES)
            qp, kp, vp = q[:, cols], kw[:, cols], vw[:, cols]
            zero = jnp.zeros_like(qp)
            qs = jnp.concatenate([jnp.where(low_half, qp, zero), jnp.where(high_half, qp, zero)], axis=0)
            s = _dot_nt(qs, kp)
            ps = []
            for e in range(2):
                se = s[e * sq:(e + 1) * sq, :] + bias
                m = jnp.max(se, axis=-1, keepdims=True)
                ps.append(jnp.exp2(se - m).astype(BF16))
                lse_ref[rows, 2 * j + e:2 * j + e + 1] = m
            o = _dot(jnp.concatenate(ps, axis=0), jnp.concatenate([vp, ones], axis=1))
            for e in range(2):
                h = A_HEADS + 2 * j + e
                lse_ref[rows, h:h + 1] = o[e * sq:(e + 1) * sq, LANES + h:LANES + h + 1]
            o_ref[rows, cols] = jnp.where(low_half, o[0:sq, 0:LANES], o[sq:, 0:LANES]).astype(BF16)


def _attention(view, dil):
    Bn, L, _ = view.shape
    qb = min(ATT_QB, L)
    nb = L // qb
    r128 = qb // ATT_Q
    last128 = L // ATT_Q - 1

    row_w = 3 * A_WIDTH
    cur = pl.BlockSpec((None, qb, row_w), lambda b, r, n: (b, n, r))
    prev = pl.BlockSpec((None, ATT_Q, row_w), lambda b, r, n: (b, jnp.maximum(n * r128 - 1, 0), r))
    nxt = pl.BlockSpec((None, ATT_Q, row_w),
                       lambda b, r, n: (b, jnp.minimum((n + 1) * r128, last128), r))

    return pl.pallas_call(
        functools.partial(_attn_kernel, qb=qb, sub_len=L),
        grid=(Bn, dil, nb),
        in_specs=[cur, prev, nxt],
        out_specs=[
            pl.BlockSpec((None, qb, A_WIDTH), lambda b, r, n: (b, n, r)),
            pl.BlockSpec((None, qb, LANES), lambda b, r, n: (b, n, r)),
        ],
        out_shape=[
            jax.ShapeDtypeStruct((Bn, L, dil * A_WIDTH), BF16),
            jax.ShapeDtypeStruct((Bn, L, dil * LANES), F32),
        ],
        scratch_shapes=[
            pltpu.VMEM((qb + 2 * ATT_Q, A_WIDTH), BF16),
            pltpu.VMEM((qb + 2 * ATT_Q, A_WIDTH), BF16),
        ],
        compiler_params=pltpu.CompilerParams(
            dimension_semantics=("arbitrary", "arbitrary", "arbitrary"),
            vmem_limit_bytes=VMEM_LIMIT_BYTES),
        name=f"attn_d{dil}",
    )(view, view, view)


def _ssd_chunk(xs, bm, cm, dt_rawT, bias_col, alog_col, state_ref, reverse, between_pairs=None):
    L = CHUNK
    dtT = _softplus(dt_rawT + bias_col)
    dtAT = dtT * (-jnp.exp(alog_col))
    ri = lax.broadcasted_iota(jnp.int32, (L, L), 0)
    ci = lax.broadcasted_iota(jnp.int32, (L, L), 1)
    tri = (ci >= ri) if reverse else (ci <= ri)
    tri_bf = jnp.where(tri, 1.0, 0.0).astype(BF16)
    acumT = sum(_dot_nt(t, tri_bf) for t in _split3(dtAT))
    e = 0 if reverse else L - 1
    edge_col = acumT[:, e:e + 1]
    wT = jnp.exp(edge_col - acumT) * dtT
    cdec = jnp.exp(edge_col)
    a2T = acumT * LOG2E
    a2 = a2T.T
    srcT = a2T - jnp.log2(dtT)
    lane = lax.broadcasted_iota(jnp.int32, (L, LANES), 1)
    low_half = lane < HEAD_DIM
    high_half = jnp.logical_not(low_half)

    def block_diag(t):
        zero = jnp.zeros_like(t)
        return jnp.concatenate([jnp.where(low_half, t, zero), jnp.where(high_half, t, zero)], axis=0)

    ys = []
    for g in range(C_GROUPS):
        bg = bm[:, g * C_STATE:(g + 1) * C_STATE]
        cg = cm[:, g * C_STATE:(g + 1) * C_STATE]
        cb = _dot_nt(cg, bg).astype(BF16)
        bgT = bg.astype(F32).T.astype(BF16)
        for pr in range(4):
            pair = g * 4 + pr
            cols = slice(pair * LANES, (pair + 1) * LANES)
            xbd = block_diag(xs[:, cols])
            prev = state_ref[:, cols]
            pbd = block_diag(prev.astype(BF16))
            ws, css, bss, cd = [], [], [], []
            for k in range(2):
                h = 2 * pair + k
                col = jnp.broadcast_to(a2[:, h:h + 1], (L, L))
                dec = jnp.exp2(jnp.where(tri, col - srcT[h:h + 1, :], NEG))
                ws.append(cb * dec.astype(BF16))
                css.append(cg * jnp.exp2(col).astype(BF16))
                bss.append(bgT * jnp.broadcast_to(wT[h:h + 1, :], (C_STATE, L)).astype(BF16))
                cd.append(cdec[h:h + 1, :])
            ys.append(_dot(jnp.concatenate(ws + css, axis=1), jnp.concatenate([xbd, pbd], axis=0)))
            decay = jnp.where(low_half[0:1, :], cd[0], cd[1])
            state_ref[:, cols] = prev * decay + _dot(jnp.concatenate(bss, axis=1), xbd)
            if between_pairs is not None:
                between_pairs(pair)
    return ys


def _mix_out_kernel(xact_ref, dtT_ref, yf_ref, zc_ref, zab_ref,
                    o1_ref, o4_ref, o16_ref, l1_ref, l4_ref, l16_ref, h_ref,
                    bcol_ref, acol_ref,
                    sguw_ref, sgub_ref, e8_ref, nrm_ref, wout_ref, fnw_ref,
                    out_ref, state_ref, ybuf_ref, mixed_ref, onat_ref, snat_ref, *, ts, final):
    s = pl.program_id(1)

    @pl.when(s == 0)
    def _():
        state_ref[...] = jnp.zeros_like(state_ref)

    slabs = A_WIDTH // LANES
    for ci, (d, o_ref, l_ref) in enumerate(((ATT_CFGS[1][1], o4_ref, l4_ref),
                                            (ATT_CFGS[2][1], o16_ref, l16_ref))):
        for r in range(d):
            dst_rows = pl.ds(r, ts // d, stride=d)
            for j in range(slabs):
                c0 = r * A_WIDTH + j * LANES
                onat_ref[ci, j, dst_rows, :] = o_ref[:, c0:c0 + LANES].astype(F32)
            snat_ref[ci, dst_rows, :] = l_ref[:, r * LANES:(r + 1) * LANES]

    bcol, acol = bcol_ref[C_HEADS:DT_COLS, :], acol_ref[C_HEADS:DT_COLS, :]
    nch = ts // CHUNK

    def out_proj_piece(rows, q):
        cols = slice(q * MXU_COLS, (q + 1) * MXU_COLS)
        out_ref[rows, cols] = h_ref[rows, cols] + _dot(mixed_ref[rows, :], wout_ref[:, cols])

    def out_proj_finish(rows):
        if final:
            hn = out_ref[rows, :]
            ms = jnp.mean(hn * hn, axis=-1, keepdims=True)
            out_ref[rows, :] = hn * lax.rsqrt(ms + EPS) * fnw_ref[...]

    n_pieces = D_MODEL // MXU_COLS
    pairs_per_piece = (C_HEADS // 2) // n_pieces

    def mix_chunk(c, done_rows):
        rows = slice(c * CHUNK, (c + 1) * CHUNK)

        def between_pairs(pair):
            if done_rows is not None and pair % pairs_per_piece == pairs_per_piece - 1:
                out_proj_piece(done_rows, pair // pairs_per_piece)

        xs = xact_ref[rows, 0:C_WIDTH]
        bm = xact_ref[rows, C_WIDTH:C_WIDTH + C_GROUPS * C_STATE]
        cm = xact_ref[rows, C_WIDTH + C_GROUPS * C_STATE:C_CONV_CH]
        ys = _ssd_chunk(xs, bm, cm, dtT_ref[c, C_HEADS:DT_COLS, :], bcol, acol, state_ref, reverse=True,
                        between_pairs=between_pairs)
        if done_rows is not None:
            out_proj_finish(done_rows)
        ssq = jnp.zeros((CHUNK, 1), F32)
        for pair, y in enumerate(ys):
            cols = slice(pair * LANES, (pair + 1) * LANES)
            y = (y + yf_ref[rows, cols].astype(F32)) * _silu(zc_ref[rows, cols].astype(F32))
            ybuf_ref[rows, cols] = y
            ssq = ssq + jnp.sum(y * y, axis=-1, keepdims=True)
        inv = lax.rsqrt(ssq * (1.0 / C_WIDTH) + EPS)
        mixed_ref[rows, A_WIDTH + B_WIDTH:D_MIX] = (ybuf_ref[rows, :] * inv * nrm_ref[...]).astype(BF16)

        st = [l1_ref[rows, :], snat_ref[0, rows, :], snat_ref[1, rows, :]]
        mx = jnp.maximum(jnp.maximum(st[0], st[1]), st[2])
        es = [jnp.exp2(t - mx) for t in st]
        dens = [pltpu.roll(t, LANES - A_HEADS, 1) for t in st]
        z = es[0] * dens[0] + es[1] * dens[1] + es[2] * dens[2]
        head_lane = lax.broadcasted_iota(jnp.int32, (CHUNK, LANES), 1) < A_HEADS
        rz = 1.0 / jnp.where(head_lane, z, 1.0)
        wexp = [_dot((ec * rz).astype(BF16), e8_ref[...]) for ec in es]
        for j in range(A_WIDTH // LANES):
            cols = slice(j * LANES, (j + 1) * LANES)
            oa = (wexp[0][:, cols] * o1_ref[rows, cols].astype(F32)
                  + wexp[1][:, cols] * onat_ref[0, j, rows, :]
                  + wexp[2][:, cols] * onat_ref[1, j, rows, :])
            za = zab_ref[rows, cols].astype(F32)
            mixed_ref[rows, cols] = (oa * _silu(za)).astype(BF16)

        for g in range(B_GROUPS):
            cols = slice(g * LANES, (g + 1) * LANES)
            ub = zab_ref[rows, A_WIDTH + g * LANES:A_WIDTH + (g + 1) * LANES].astype(F32)
            vb = zab_ref[rows, 2 * A_WIDTH + g * LANES:2 * A_WIDTH + (g + 1) * LANES]
            zb = zab_ref[rows, 3 * A_WIDTH + g * LANES:3 * A_WIDTH + (g + 1) * LANES].astype(F32)
            mixed = _dot(sguw_ref[g], vb) + sgub_ref[:, cols]
            mixed_ref[rows, A_WIDTH + g * LANES:A_WIDTH + (g + 1) * LANES] = (
                ub * mixed * _silu(zb)).astype(BF16)

        return rows

    done = None
    for c in reversed(range(nch)):
        done = mix_chunk(c, done)
    for q in range(n_pieces):
        out_proj_piece(done, q)
    out_proj_finish(done)


def _mix_out(xact3, dtT, yf3, zc3, zab3, o1, o4, o16, l1, l4, l16, h3,
             bcol, acol, sguw, sgub, e8, nrm, wout, fnw, final):
    Bn, S, _ = xact3.shape
    ts = TS_SEQ
    ns = S // ts
    const2 = lambda b, s: (0, 0)
    const3 = lambda b, s: (0, 0, 0)
    tile = lambda b, s: (b, ns - 1 - s, 0)
    tspec = lambda w: pl.BlockSpec((None, ts, w), tile)
    gspec = lambda d, w: pl.BlockSpec((None, ts // d, d * w), tile)
    d4, d16 = ATT_CFGS[1][1], ATT_CFGS[2][1]
    return pl.pallas_call(
        functools.partial(_mix_out_kernel, ts=ts, final=final),
        grid=(Bn, ns),
        in_specs=[
            tspec(C_CONV_CH),
            pl.BlockSpec((ts // CHUNK, DT_COLS, CHUNK), lambda b, s: (b * ns + ns - 1 - s, 0, 0)),
            tspec(C_WIDTH), tspec(C_WIDTH), tspec(2048),
            tspec(A_WIDTH), gspec(d4, A_WIDTH), gspec(d16, A_WIDTH),
            tspec(LANES), gspec(d4, LANES), gspec(d16, LANES),
            tspec(D_MODEL),
            pl.BlockSpec((DT_COLS, 1), const2),
            pl.BlockSpec((DT_COLS, 1), const2),
            pl.BlockSpec((B_GROUPS, CHUNK, CHUNK), const3),
            pl.BlockSpec((CHUNK, B_WIDTH), const2),
            pl.BlockSpec((LANES, A_WIDTH), const2),
            pl.BlockSpec((1, C_WIDTH), const2),
            pl.BlockSpec((D_MIX, D_MODEL), const2),
            pl.BlockSpec((1, D_MODEL), const2),
        ],
        out_specs=pl.BlockSpec((None, ts, D_MODEL), tile),
        out_shape=jax.ShapeDtypeStruct((Bn, S, D_MODEL), F32),
        scratch_shapes=[
            pltpu.VMEM((C_STATE, C_WIDTH), F32),
            pltpu.VMEM((ts, C_WIDTH), F32),
            pltpu.VMEM((ts, D_MIX), BF16),
            pltpu.VMEM((2, A_WIDTH // LANES, ts, LANES), F32),
            pltpu.VMEM((2, ts, LANES), F32),
        ],
        compiler_params=pltpu.CompilerParams(
            dimension_semantics=("arbitrary", "arbitrary"), vmem_limit_bytes=VMEM_LIMIT_BYTES),
        name="mix_out_final" if final else "mix_out",
    )(xact3, dtT, yf3, zc3, zab3, o1, o4, o16, l1, l4, l16, h3,
      bcol, acol, sguw, sgub, e8, nrm, wout, fnw)


def _rope_tables(S):
    inv = ROPE_THETA ** (-jnp.arange(0, HEAD_DIM, 2, dtype=F32) / HEAD_DIM)
    ang = jnp.arange(S, dtype=F32)[:, None] * inv[None, :]
    cos, sin = jnp.cos(ang), jnp.sin(ang)
    cos_t = jnp.concatenate([cos, cos, cos, cos], axis=-1)
    sin_t = jnp.concatenate([-sin, sin, -sin, sin], axis=-1)
    return cos_t, sin_t


def _layer_params(l, norm_w, w_in, sgu_w, sgu_b, conv_w, conv_b, dt_bias, a_log, d_skip,
                  ssd_norm_w, w_out):
    w = w_in[l]
    head_of_lane = jnp.arange(A_WIDTH) // HEAD_DIM
    e8 = (jnp.arange(LANES)[:, None] == head_of_lane[None, :]).astype(BF16)
    return dict(
        nw=norm_w[l][None, :],
        w_main=w[:, :MAIN_COLS].astype(BF16),
        w_dtT=w[:, MAIN_COLS:].T.astype(BF16),
        sguw=sgu_w[l].astype(BF16),
        sgub=jnp.repeat(sgu_b[l].T, B_WIDTH // B_GROUPS, axis=1),
        conv_w=conv_w[l],
        conv_b=conv_b[l][None, :],
        bcol=dt_bias[l].reshape(DT_COLS, 1),
        acol=a_log[l].reshape(DT_COLS, 1),
        dskip=jnp.repeat(d_skip[l], HEAD_DIM)[None, :],
        nrm=ssd_norm_w[l][None, :],
        wout=w_out[l].astype(BF16),
        e8=e8,
    )


def _trunk(x, layers, fnw):
    Bn, S, _ = x.shape
    T = Bn * S
    assert S % (ATT_Q * max(d for _, d in ATT_CFGS)) == 0
    assert S % TS_SEQ == 0 and S % TM_PROJ == 0
    cos_t, sin_t = _rope_tables(S)
    h = x
    for li, p in enumerate(layers):
        qkv, qkv4, qkv16, zab, xact, zc, dtT, yf = _in_proj(
            h.reshape(T, D_MODEL), S, p["nw"], cos_t, sin_t, p["w_main"], p["w_dtT"],
            p["conv_w"], p["conv_b"], p["bcol"], p["acol"], p["dskip"])
        att = [_attention(t.reshape(Bn, S // dil, dil * 3 * A_WIDTH), dil)
               for t, (_, dil) in zip((qkv, qkv4, qkv16), ATT_CFGS)]
        xact3 = xact.reshape(Bn, S, C_CONV_CH)
        h = _mix_out(xact3, dtT, yf.reshape(Bn, S, C_WIDTH), zc.reshape(Bn, S, C_WIDTH),
                     zab.reshape(Bn, S, 2048),
                     att[0][0], att[1][0], att[2][0], att[0][1], att[1][1], att[2][1], h,
                     p["bcol"], p["acol"], p["sguw"], p["sgub"], p["e8"],
                     p["nrm"], p["wout"], fnw, final=(li == len(layers) - 1))
    return h


def kernel(x_prompt, x_sample, norm_w, w_in, sgu_w, sgu_b, conv_w, conv_b, dt_bias, a_log, d_skip,
           ssd_norm_w, w_out, final_norm_w):
    depth = w_in.shape[0]
    layers = [_layer_params(l, norm_w, w_in, sgu_w, sgu_b, conv_w, conv_b, dt_bias, a_log, d_skip,
                            ssd_norm_w, w_out) for l in range(depth)]
    fnw = final_norm_w[None, :]
    return (_trunk(x_prompt, layers, fnw), _trunk(x_sample, layers, fnw))
```

```python
import functools

import jax
import jax.numpy as jnp
from jax import lax
from jax.experimental import pallas as pl
from jax.experimental.pallas import tpu as pltpu

F32 = jnp.float32
BF16 = jnp.bfloat16

D_MODEL = 1024
HEAD_DIM = 64
A_WIDTH = 512
A_HEADS = 8
ATT_CFGS = ((128, 1), (512, 4), (2048, 16))
ATT_Q = 128
ATT_HALF = 64
ROPE_THETA = 10000.0
B_WIDTH = 512
B_GROUPS = 4
CHUNK = 128
C_WIDTH = 1024
C_HEADS = 16
C_STATE = 128
C_GROUPS = 2
C_CONV_CH = C_WIDTH + 2 * C_GROUPS * C_STATE
MAIN_COLS = 6144
DT_COLS = 2 * C_HEADS
D_MIX = 2048
EPS = 1e-5
NEG = -1e30
LOG2E = 1.4426950408889634
LN2 = 0.6931471805599453

VMEM_LIMIT_BYTES = 56 * 1024 * 1024
LANES = 128
F32_SUBLANES = 8
EDGE_ROWS = 16
MXU_COLS = 256

TM_PROJ = 512
TS_SEQ = 512
ATT_QB = 2048
ATT_SUBQ = 64


def _silu(x):
    hx = 0.5 * x
    return hx + hx * jnp.tanh(hx)


def _softplus(x):
    return jnp.maximum(x, 0.0) + jnp.log1p(jnp.exp(-jnp.abs(x)))


def _split3(x):
    hi = x.astype(BF16)
    r1 = x - hi.astype(F32)
    mid = r1.astype(BF16)
    lo = (r1 - mid.astype(F32)).astype(BF16)
    return hi, mid, lo


def _dot(a, b):
    return jnp.dot(a, b, preferred_element_type=F32)


def _dot_nt(a, b):
    return lax.dot_general(a, b, (((1,), (1,)), ((), ())), preferred_element_type=F32)


def _inproj_kernel(x_ref, xprev_ref, xnext_ref, nw_ref, cos_ref, sin_ref, w_ref, wdtT_ref,
                   cw_ref, cb_ref, bcol_ref, acol_ref, dskip_ref,
                   qkv_ref, qkv4_ref, qkv16_ref, zab_ref, xact_ref, zc_ref, dtT_ref, yf_ref,
                   qkv32_ref, by4_ref, state_ref, *, tm, tiles_per_seq):
    ti = pl.program_id(0) % tiles_per_seq
    d4, d16 = ATT_CFGS[1][1], ATT_CFGS[2][1]

    @pl.when(ti == 0)
    def _():
        state_ref[...] = jnp.zeros_like(state_ref)

    x = jnp.concatenate([x_ref[...], xprev_ref[...], xnext_ref[...]], axis=0)
    ms = jnp.mean(x * x, axis=-1, keepdims=True)
    xn_ext = (x * lax.rsqrt(ms + EPS) * nw_ref[...]).astype(BF16)
    xn = xn_ext[0:tm, :]
    cos = cos_ref[...]
    sin = sin_ref[...]
    lane = lax.broadcasted_iota(jnp.int32, cos.shape, 1)
    first_half = (lane % HEAD_DIM) < (HEAD_DIM // 2)
    gw = MXU_COLS
    gslabs = gw // LANES
    qk_cols, qkv_cols = 2 * A_WIDTH, 3 * A_WIDTH
    zab_end, xbc_end = qkv_cols + 2048, qkv_cols + 2048 + C_CONV_CH
    erow = lax.broadcasted_iota(jnp.int32, (EDGE_ROWS, gw), 0)

    def emit_group(c0):
        if c0 < qkv_cols:
            acc = _dot(xn, w_ref[:, c0:c0 + gw])
            scale = HEAD_DIM ** -0.5 * LOG2E if c0 < A_WIDTH else 1.0
            for j in range(gslabs):
                slab = c0 // LANES + j
                out = acc[:, j * LANES:(j + 1) * LANES]
                if c0 < qk_cols:
                    fwd = pltpu.roll(out, HEAD_DIM // 2, 1)
                    bwd = pltpu.roll(out, LANES - HEAD_DIM // 2, 1)
                    out = out * cos + jnp.where(first_half, bwd, fwd) * sin
                    if scale != 1.0:
                        out = out * scale
                qkv32_ref[slab] = out
                qkv_ref[:, slab * LANES:(slab + 1) * LANES] = out.astype(BF16)
                n4 = tm // d4
                for r in range(d4):
                    piece = qkv32_ref[slab, pl.ds(r, n4, stride=d4), :]
                    by4_ref[slab, r * n4:(r + 1) * n4, :] = piece
                    c = r * qkv_cols + slab * LANES
                    qkv4_ref[:, c:c + LANES] = piece.astype(BF16)
                for r in range(d16):
                    piece = by4_ref[slab, pl.ds((r % d4) * n4 + r // d4, tm // d16, stride=d16 // d4), :]
                    c = r * qkv_cols + slab * LANES
                    qkv16_ref[:, c:c + LANES] = piece.astype(BF16)
        elif c0 < zab_end:
            zab_ref[:, c0 - qkv_cols:c0 - qkv_cols + gw] = _dot(xn, w_ref[:, c0:c0 + gw]).astype(BF16)
        elif c0 < xbc_end:
            cols = slice(c0 - zab_end, c0 - zab_end + gw)
            ext = _dot(xn_ext, w_ref[:, c0:c0 + gw])
            cur = ext[0:tm, :]
            prev_row = jnp.where(ti > 0, ext[tm + 7:tm + 8, :], 0.0)
            next_row = jnp.where(ti < tiles_per_seq - 1, ext[tm + 8:tm + 9, :], 0.0)

            def conv_act(xm1, x0, xp1):
                conv = (cw_ref[0:1, cols] * xm1 + cw_ref[1:2, cols] * x0 + cw_ref[2:3, cols] * xp1
                        + cb_ref[:, cols])
                return _silu(conv).astype(BF16)

            xact_ref[:, cols] = conv_act(pltpu.roll(cur, 1, 0), cur, pltpu.roll(cur, tm - 1, 0))
            e = EDGE_ROWS
            top, bot = cur[0:e, :], cur[tm - e:tm, :]
            first, last = erow == 0, erow == e - 1
            top_m1 = jnp.where(first, prev_row, pltpu.roll(top, 1, 0))
            top_p1 = jnp.where(last, cur[e:e + 1, :], pltpu.roll(top, e - 1, 0))
            xact_ref[0:e, cols] = conv_act(top_m1, top, top_p1)
            bot_m1 = jnp.where(first, cur[tm - e - 1:tm - e, :], pltpu.roll(bot, 1, 0))
            bot_p1 = jnp.where(last, next_row, pltpu.roll(bot, e - 1, 0))
            xact_ref[tm - e:tm, cols] = conv_act(bot_m1, bot, bot_p1)
        else:
            zc_ref[:, c0 - xbc_end:c0 - xbc_end + gw] = _dot(xn, w_ref[:, c0:c0 + gw]).astype(BF16)

    dtT = _dot_nt(wdtT_ref[...], xn)
    nch = tm // CHUNK
    for c in range(nch):
        dtT_ref[c] = dtT[:, c * CHUNK:(c + 1) * CHUNK]

    groups = list(range(0, MAIN_COLS, gw))
    conv_groups = [c for c in groups if zab_end <= c < xbc_end]
    qkv_groups = [c for c in groups if c < qkv_cols]
    plain = [c for c in groups if c not in conv_groups and c not in qkv_groups]
    for heavy, light in zip(conv_groups, plain):
        emit_group(heavy)
        emit_group(light)
    plain = plain[len(conv_groups):]
    later = [c for pair in zip(qkv_groups, plain) for c in pair]
    assert len(qkv_groups) == len(plain)
    slots = nch * (C_HEADS // 2)
    emitted = [0]

    def between_pairs_of(chunk):
        def hook(pair):
            slot = chunk * (C_HEADS // 2) + pair + 1
            while emitted[0] < len(later) and emitted[0] * slots < slot * len(later):
                emit_group(later[emitted[0]])
                emitted[0] += 1
        return hook

    bcol, acol = bcol_ref[0:C_HEADS, :], acol_ref[0:C_HEADS, :]
    for c in range(nch):
        rows = slice(c * CHUNK, (c + 1) * CHUNK)
        xs = xact_ref[rows, 0:C_WIDTH]
        bm = xact_ref[rows, C_WIDTH:C_WIDTH + C_GROUPS * C_STATE]
        cm = xact_ref[rows, C_WIDTH + C_GROUPS * C_STATE:C_CONV_CH]
        ys = _ssd_chunk(xs, bm, cm, dtT[0:C_HEADS, rows], bcol, acol, state_ref, reverse=False,
                        between_pairs=between_pairs_of(c))
        for pair, y in enumerate(ys):
            cols = slice(pair * LANES, (pair + 1) * LANES)
            y = y + dskip_ref[:, cols] * xs[:, cols].astype(F32)
            yf_ref[rows, cols] = y.astype(BF16)
    assert emitted[0] == len(later)


def _in_proj(x2d, seq_len, nw, cos_t, sin_t, w_main, w_dtT, conv_w, conv_b, bcol, acol, dskip):
    T = x2d.shape[0]
    tm = TM_PROJ
    tiles_per_seq = seq_len // tm
    hb = tm // F32_SUBLANES
    last_hb = T // F32_SUBLANES - 1
    d4, d16 = ATT_CFGS[1][1], ATT_CFGS[2][1]
    const = lambda i: (0, 0)
    row = lambda i: (i, 0)
    return pl.pallas_call(
        functools.partial(_inproj_kernel, tm=tm, tiles_per_seq=tiles_per_seq),
        grid=(T // tm,),
        in_specs=[
            pl.BlockSpec((tm, D_MODEL), row),
            pl.BlockSpec((F32_SUBLANES, D_MODEL), lambda i: (jnp.maximum(i * hb - 1, 0), 0)),
            pl.BlockSpec((F32_SUBLANES, D_MODEL), lambda i: (jnp.minimum((i + 1) * hb, last_hb), 0)),
            pl.BlockSpec((1, D_MODEL), const),
            pl.BlockSpec((tm, LANES), lambda i: (i % tiles_per_seq, 0)),
            pl.BlockSpec((tm, LANES), lambda i: (i % tiles_per_seq, 0)),
            pl.BlockSpec((D_MODEL, MAIN_COLS), const),
            pl.BlockSpec((DT_COLS, D_MODEL), const),
            pl.BlockSpec((3, C_CONV_CH), const),
            pl.BlockSpec((1, C_CONV_CH), const),
            pl.BlockSpec((DT_COLS, 1), const),
            pl.BlockSpec((DT_COLS, 1), const),
            pl.BlockSpec((1, C_WIDTH), const),
        ],
        out_specs=[
            pl.BlockSpec((tm, 3 * A_WIDTH), row),
            pl.BlockSpec((tm // d4, d4 * 3 * A_WIDTH), row),
            pl.BlockSpec((tm // d16, d16 * 3 * A_WIDTH), row),
            pl.BlockSpec((tm, 2048), row),
            pl.BlockSpec((tm, C_CONV_CH), row),
            pl.BlockSpec((tm, C_WIDTH), row),
            pl.BlockSpec((tm // CHUNK, DT_COLS, CHUNK), lambda i: (i, 0, 0)),
            pl.BlockSpec((tm, C_WIDTH), row),
        ],
        out_shape=[
            jax.ShapeDtypeStruct((T, 3 * A_WIDTH), BF16),
            jax.ShapeDtypeStruct((T // d4, d4 * 3 * A_WIDTH), BF16),
            jax.ShapeDtypeStruct((T // d16, d16 * 3 * A_WIDTH), BF16),
            jax.ShapeDtypeStruct((T, 2048), BF16),
            jax.ShapeDtypeStruct((T, C_CONV_CH), BF16),
            jax.ShapeDtypeStruct((T, C_WIDTH), BF16),
            jax.ShapeDtypeStruct((T // CHUNK, DT_COLS, CHUNK), F32),
            jax.ShapeDtypeStruct((T, C_WIDTH), BF16),
        ],
        scratch_shapes=[pltpu.VMEM((3 * A_WIDTH // LANES, tm, LANES), F32),
                        pltpu.VMEM((3 * A_WIDTH // LANES, tm, LANES), F32),
                        pltpu.VMEM((C_STATE, C_WIDTH), F32)],
        compiler_params=pltpu.CompilerParams(
            dimension_semantics=("arbitrary",), vmem_limit_bytes=VMEM_LIMIT_BYTES),
        name="in_proj",
    )(x2d, x2d, x2d, nw, cos_t, sin_t, w_main, w_dtT, conv_w, conv_b, bcol, acol, dskip)


def _attn_kernel(cur_ref, prev_ref, next_ref, o_ref, lse_ref, kbuf, vbuf, *, qb, sub_len):
    n = pl.program_id(2)
    kcols, vcols = slice(A_WIDTH, 2 * A_WIDTH), slice(2 * A_WIDTH, 3 * A_WIDTH)
    kbuf[0:ATT_Q, :] = prev_ref[:, kcols]
    kbuf[ATT_Q:ATT_Q + qb, :] = cur_ref[:, kcols]
    kbuf[ATT_Q + qb:, :] = next_ref[:, kcols]
    vbuf[0:ATT_Q, :] = prev_ref[:, vcols]
    vbuf[ATT_Q:ATT_Q + qb, :] = cur_ref[:, vcols]
    vbuf[ATT_Q + qb:, :] = next_ref[:, vcols]

    sq = ATT_SUBQ
    nkeys = MXU_COLS
    assert sq + 2 * ATT_HALF <= nkeys
    qq = lax.broadcasted_iota(jnp.int32, (sq, nkeys), 0)
    kk = lax.broadcasted_iota(jnp.int32, (sq, nkeys), 1)
    kk1 = lax.broadcasted_iota(jnp.int32, (1, nkeys), 1)
    lane = lax.broadcasted_iota(jnp.int32, (sq, LANES), 1)
    low_half = lane < HEAD_DIM
    high_half = jnp.logical_not(low_half)
    band_bias = jnp.where((kk >= qq) & (kk <= qq + 2 * ATT_HALF), 0.0, NEG)
    ones = jnp.ones((nkeys, LANES), BF16)

    for i in range(qb // sq):
        r0 = i * sq
        k0 = i * sq + ATT_Q - ATT_HALF
        q = cur_ref[r0:r0 + sq, 0:A_WIDTH]
        kw = kbuf[k0:k0 + nkeys, :]
        vw = vbuf[k0:k0 + nkeys, :]
        pos = kk1 + (n * qb + i * sq - ATT_HALF)
        bias = band_bias + jnp.where((pos >= 0) & (pos < sub_len), 0.0, NEG)
        rows = slice(r0, r0 + sq)
        lse_ref[rows, :] = jnp.zeros((sq, LANES), F32)
        for j in range(A_WIDTH // LANES):
            cols = slice(j * LANES, (j + 1) * LANES)
            qp, kp, vp = q[:, cols], kw[:, cols], vw[:, cols]
            zero = jnp.zeros_like(qp)
            qs = jnp.concatenate([jnp.where(low_half, qp, zero), jnp.where(high_half, qp, zero)], axis=0)
            s = _dot_nt(qs, kp)
            ps = []
            for e in range(2):
                se = s[e * sq:(e + 1) * sq, :] + bias
                m = jnp.max(se, axis=-1, keepdims=True)
                ps.append(jnp.exp2(se - m).astype(BF16))
                lse_ref[rows, 2 * j + e:2 * j + e + 1] = m
            o = _dot(jnp.concatenate(ps, axis=0), jnp.concatenate([vp, ones], axis=1))
            for e in range(2):
                h = A_HEADS + 2 * j + e
                lse_ref[rows, h:h + 1] = o[e * sq:(e + 1) * sq, LANES + h:LANES + h + 1]
            o_ref[rows, cols] = jnp.where(low_half, o[0:sq, 0:LANES], o[sq:, 0:LANES]).astype(BF16)


def _attention(view, dil):
    Bn, L, _ = view.shape
    qb = min(ATT_QB, L)
    nb = L // qb
    r128 = qb // ATT_Q
    last128 = L // ATT_Q - 1

    row_w = 3 * A_WIDTH
    cur = pl.BlockSpec((None, qb, row_w), lambda b, r, n: (b, n, r))
    prev = pl.BlockSpec((None, ATT_Q, row_w), lambda b, r, n: (b, jnp.maximum(n * r128 - 1, 0), r))
    nxt = pl.BlockSpec((None, ATT_Q, row_w),
                       lambda b, r, n: (b, jnp.minimum((n + 1) * r128, last128), r))

    return pl.pallas_call(
        functools.partial(_attn_kernel, qb=qb, sub_len=L),
        grid=(Bn, dil, nb),
        in_specs=[cur, prev, nxt],
        out_specs=[
            pl.BlockSpec((None, qb, A_WIDTH), lambda b, r, n: (b, n, r)),
            pl.BlockSpec((None, qb, LANES), lambda b, r, n: (b, n, r)),
        ],
        out_shape=[
            jax.ShapeDtypeStruct((Bn, L, dil * A_WIDTH), BF16),
            jax.ShapeDtypeStruct((Bn, L, dil * LANES), F32),
        ],
        scratch_shapes=[
            pltpu.VMEM((qb + 2 * ATT_Q, A_WIDTH), BF16),
            pltpu.VMEM((qb + 2 * ATT_Q, A_WIDTH), BF16),
        ],
        compiler_params=pltpu.CompilerParams(
            dimension_semantics=("arbitrary", "arbitrary", "arbitrary"),
            vmem_limit_bytes=VMEM_LIMIT_BYTES),
        name=f"attn_d{dil}",
    )(view, view, view)


def _ssd_chunk(xs, bm, cm, dt_rawT, bias_col, alog_col, state_ref, reverse, between_pairs=None):
    L = CHUNK
    dtT = _softplus(dt_rawT + bias_col)
    dtAT = dtT * (-jnp.exp(alog_col))
    ri = lax.broadcasted_iota(jnp.int32, (L, L), 0)
    ci = lax.broadcasted_iota(jnp.int32, (L, L), 1)
    tri = (ci >= ri) if reverse else (ci <= ri)
    tri_bf = jnp.where(tri, 1.0, 0.0).astype(BF16)
    acumT = sum(_dot_nt(t, tri_bf) for t in _split3(dtAT))
    e = 0 if reverse else L - 1
    edge_col = acumT[:, e:e + 1]
    wT = jnp.exp(edge_col - acumT) * dtT
    cdec = jnp.exp(edge_col)
    a2T = acumT * LOG2E
    a2 = a2T.T
    srcT = a2T - jnp.log2(dtT)
    lane = lax.broadcasted_iota(jnp.int32, (L, LANES), 1)
    low_half = lane < HEAD_DIM
    high_half = jnp.logical_not(low_half)

    def block_diag(t):
        zero = jnp.zeros_like(t)
        return jnp.concatenate([jnp.where(low_half, t, zero), jnp.where(high_half, t, zero)], axis=0)

    ys = []
    for g in range(C_GROUPS):
        bg = bm[:, g * C_STATE:(g + 1) * C_STATE]
        cg = cm[:, g * C_STATE:(g + 1) * C_STATE]
        cb = _dot_nt(cg, bg).astype(BF16)
        bgT = bg.astype(F32).T.astype(BF16)
        for pr in range(4):
            pair = g * 4 + pr
            cols = slice(pair * LANES, (pair + 1) * LANES)
            xbd = block_diag(xs[:, cols])
            prev = state_ref[:, cols]
            pbd = block_diag(prev.astype(BF16))
            ws, css, bss, cd = [], [], [], []
            for k in range(2):
                h = 2 * pair + k
                col = jnp.broadcast_to(a2[:, h:h + 1], (L, L))
                dec = jnp.exp2(jnp.where(tri, col - srcT[h:h + 1, :], NEG))
                ws.append(cb * dec.astype(BF16))
                css.append(cg * jnp.exp2(col).astype(BF16))
                bss.append(bgT * jnp.broadcast_to(wT[h:h + 1, :], (C_STATE, L)).astype(BF16))
                cd.append(cdec[h:h + 1, :])
            ys.append(_dot(jnp.concatenate(ws + css, axis=1), jnp.concatenate([xbd, pbd], axis=0)))
            decay = jnp.where(low_half[0:1, :], cd[0], cd[1])
            state_ref[:, cols] = prev * decay + _dot(jnp.concatenate(bss, axis=1), xbd)
            if between_pairs is not None:
                between_pairs(pair)
    return ys


def _mix_out_kernel(xact_ref, dtT_ref, yf_ref, zc_ref, zab_ref,
                    o1_ref, o4_ref, o16_ref, l1_ref, l4_ref, l16_ref, h_ref,
                    bcol_ref, acol_ref,
                    sguw_ref, sgub_ref, e8_ref, nrm_ref, wout_ref, fnw_ref,
                    out_ref, state_ref, ybuf_ref, mixed_ref, onat_ref, snat_ref, *, ts, final):
    s = pl.program_id(1)

    @pl.when(s == 0)
    def _():
        state_ref[...] = jnp.zeros_like(state_ref)

    slabs = A_WIDTH // LANES
    for ci, (d, o_ref, l_ref) in enumerate(((ATT_CFGS[1][1], o4_ref, l4_ref),
                                            (ATT_CFGS[2][1], o16_ref, l16_ref))):
        for r in range(d):
            dst_rows = pl.ds(r, ts // d, stride=d)
            for j in range(slabs):
                c0 = r * A_WIDTH + j * LANES
                onat_ref[ci, j, dst_rows, :] = o_ref[:, c0:c0 + LANES].astype(F32)
            snat_ref[ci, dst_rows, :] = l_ref[:, r * LANES:(r + 1) * LANES]

    bcol, acol = bcol_ref[C_HEADS:DT_COLS, :], acol_ref[C_HEADS:DT_COLS, :]
    nch = ts // CHUNK

    def out_proj_piece(rows, q):
        cols = slice(q * MXU_COLS, (q + 1) * MXU_COLS)
        out_ref[rows, cols] = h_ref[rows, cols] + _dot(mixed_ref[rows, :], wout_ref[:, cols])

    def out_proj_finish(rows):
        if final:
            hn = out_ref[rows, :]
            ms = jnp.mean(hn * hn, axis=-1, keepdims=True)
            out_ref[rows, :] = hn * lax.rsqrt(ms + EPS) * fnw_ref[...]

    n_pieces = D_MODEL // MXU_COLS
    pairs_per_piece = (C_HEADS // 2) // n_pieces

    def mix_chunk(c, done_rows):
        rows = slice(c * CHUNK, (c + 1) * CHUNK)

        def between_pairs(pair):
            if done_rows is not None and pair % pairs_per_piece == pairs_per_piece - 1:
                out_proj_piece(done_rows, pair // pairs_per_piece)

        xs = xact_ref[rows, 0:C_WIDTH]
        bm = xact_ref[rows, C_WIDTH:C_WIDTH + C_GROUPS * C_STATE]
        cm = xact_ref[rows, C_WIDTH + C_GROUPS * C_STATE:C_CONV_CH]
        ys = _ssd_chunk(xs, bm, cm, dtT_ref[c, C_HEADS:DT_COLS, :], bcol, acol, state_ref, reverse=True,
                        between_pairs=between_pairs)
        if done_rows is not None:
            out_proj_finish(done_rows)
        ssq = jnp.zeros((CHUNK, 1), F32)
        for pair, y in enumerate(ys):
            cols = slice(pair * LANES, (pair + 1) * LANES)
            y = (y + yf_ref[rows, cols].astype(F32)) * _silu(zc_ref[rows, cols].astype(F32))
            ybuf_ref[rows, cols] = y
            ssq = ssq + jnp.sum(y * y, axis=-1, keepdims=True)
        inv = lax.rsqrt(ssq * (1.0 / C_WIDTH) + EPS)
        mixed_ref[rows, A_WIDTH + B_WIDTH:D_MIX] = (ybuf_ref[rows, :] * inv * nrm_ref[...]).astype(BF16)

        st = [l1_ref[rows, :], snat_ref[0, rows, :], snat_ref[1, rows, :]]
        mx = jnp.maximum(jnp.maximum(st[0], st[1]), st[2])
        es = [jnp.exp2(t - mx) for t in st]
        dens = [pltpu.roll(t, LANES - A_HEADS, 1) for t in st]
        z = es[0] * dens[0] + es[1] * dens[1] + es[2] * dens[2]
        head_lane = lax.broadcasted_iota(jnp.int32, (CHUNK, LANES), 1) < A_HEADS
        rz = 1.0 / jnp.where(head_lane, z, 1.0)
        wexp = [_dot((ec * rz).astype(BF16), e8_ref[...]) for ec in es]
        for j in range(A_WIDTH // LANES):
            cols = slice(j * LANES, (j + 1) * LANES)
            oa = (wexp[0][:, cols] * o1_ref[rows, cols].astype(F32)
                  + wexp[1][:, cols] * onat_ref[0, j, rows, :]
                  + wexp[2][:, cols] * onat_ref[1, j, rows, :])
            za = zab_ref[rows, cols].astype(F32)
            mixed_ref[rows, cols] = (oa * _silu(za)).astype(BF16)

        for g in range(B_GROUPS):
            cols = slice(g * LANES, (g + 1) * LANES)
            ub = zab_ref[rows, A_WIDTH + g * LANES:A_WIDTH + (g + 1) * LANES].astype(F32)
            vb = zab_ref[rows, 2 * A_WIDTH + g * LANES:2 * A_WIDTH + (g + 1) * LANES]
            zb = zab_ref[rows, 3 * A_WIDTH + g * LANES:3 * A_WIDTH + (g + 1) * LANES].astype(F32)
            mixed = _dot(sguw_ref[g], vb) + sgub_ref[:, cols]
            mixed_ref[rows, A_WIDTH + g * LANES:A_WIDTH + (g + 1) * LANES] = (
                ub * mixed * _silu(zb)).astype(BF16)

        return rows

    done = None
    for c in reversed(range(nch)):
        done = mix_chunk(c, done)
    for q in range(n_pieces):
        out_proj_piece(done, q)
    out_proj_finish(done)


def _mix_out(xact3, dtT, yf3, zc3, zab3, o1, o4, o16, l1, l4, l16, h3,
             bcol, acol, sguw, sgub, e8, nrm, wout, fnw, final):
    Bn, S, _ = xact3.shape
    ts = TS_SEQ
    ns = S // ts
    const2 = lambda b, s: (0, 0)
    const3 = lambda b, s: (0, 0, 0)
    tile = lambda b, s: (b, ns - 1 - s, 0)
    tspec = lambda w: pl.BlockSpec((None, ts, w), tile)
    gspec = lambda d, w: pl.BlockSpec((None, ts // d, d * w), tile)
    d4, d16 = ATT_CFGS[1][1], ATT_CFGS[2][1]
    return pl.pallas_call(
        functools.partial(_mix_out_kernel, ts=ts, final=final),
        grid=(Bn, ns),
        in_specs=[
            tspec(C_CONV_CH),
            pl.BlockSpec((ts // CHUNK, DT_COLS, CHUNK), lambda b, s: (b * ns + ns - 1 - s, 0, 0)),
            tspec(C_WIDTH), tspec(C_WIDTH), tspec(2048),
            tspec(A_WIDTH), gspec(d4, A_WIDTH), gspec(d16, A_WIDTH),
            tspec(LANES), gspec(d4, LANES), gspec(d16, LANES),
            tspec(D_MODEL),
            pl.BlockSpec((DT_COLS, 1), const2),
            pl.BlockSpec((DT_COLS, 1), const2),
            pl.BlockSpec((B_GROUPS, CHUNK, CHUNK), const3),
            pl.BlockSpec((CHUNK, B_WIDTH), const2),
            pl.BlockSpec((LANES, A_WIDTH), const2),
            pl.BlockSpec((1, C_WIDTH), const2),
            pl.BlockSpec((D_MIX, D_MODEL), const2),
            pl.BlockSpec((1, D_MODEL), const2),
        ],
        out_specs=pl.BlockSpec((None, ts, D_MODEL), tile),
        out_shape=jax.ShapeDtypeStruct((Bn, S, D_MODEL), F32),
        scratch_shapes=[
            pltpu.VMEM((C_STATE, C_WIDTH), F32),
            pltpu.VMEM((ts, C_WIDTH), F32),
            pltpu.VMEM((ts, D_MIX), BF16),
            pltpu.VMEM((2, A_WIDTH // LANES, ts, LANES), F32),
            pltpu.VMEM((2, ts, LANES), F32),
        ],
        compiler_params=pltpu.CompilerParams(
            dimension_semantics=("arbitrary", "arbitrary"), vmem_limit_bytes=VMEM_LIMIT_BYTES),
        name="mix_out_final" if final else "mix_out",
    )(xact3, dtT, yf3, zc3, zab3, o1, o4, o16, l1, l4, l16, h3,
      bcol, acol, sguw, sgub, e8, nrm, wout, fnw)


def _rope_tables(S):
    inv = ROPE_THETA ** (-jnp.arange(0, HEAD_DIM, 2, dtype=F32) / HEAD_DIM)
    ang = jnp.arange(S, dtype=F32)[:, None] * inv[None, :]
    cos, sin = jnp.cos(ang), jnp.sin(ang)
    cos_t = jnp.concatenate([cos, cos, cos, cos], axis=-1)
    sin_t = jnp.concatenate([-sin, sin, -sin, sin], axis=-1)
    return cos_t, sin_t


def _layer_params(l, norm_w, w_in, sgu_w, sgu_b, conv_w, conv_b, dt_bias, a_log, d_skip,
                  ssd_norm_w, w_out):
    w = w_in[l]
    head_of_lane = jnp.arange(A_WIDTH) // HEAD_DIM
    e8 = (jnp.arange(LANES)[:, None] == head_of_lane[None, :]).astype(BF16)
    return dict(
        nw=norm_w[l][None, :],
        w_main=w[:, :MAIN_COLS].astype(BF16),
        w_dtT=w[:, MAIN_COLS:].T.astype(BF16),
        sguw=sgu_w[l].astype(BF16),
        sgub=jnp.repeat(sgu_b[l].T, B_WIDTH // B_GROUPS, axis=1),
        conv_w=conv_w[l],
        conv_b=conv_b[l][None, :],
        bcol=dt_bias[l].reshape(DT_COLS, 1),
        acol=a_log[l].reshape(DT_COLS, 1),
        dskip=jnp.repeat(d_skip[l], HEAD_DIM)[None, :],
        nrm=ssd_norm_w[l][None, :],
        wout=w_out[l].astype(BF16),
        e8=e8,
    )


def _trunk(x, layers, fnw):
    Bn, S, _ = x.shape
    T = Bn * S
    assert S % (ATT_Q * max(d for _, d in ATT_CFGS)) == 0
    assert S % TS_SEQ == 0 and S % TM_PROJ == 0
    cos_t, sin_t = _rope_tables(S)
    h = x
    for li, p in enumerate(layers):
        qkv, qkv4, qkv16, zab, xact, zc, dtT, yf = _in_proj(
            h.reshape(T, D_MODEL), S, p["nw"], cos_t, sin_t, p["w_main"], p["w_dtT"],
            p["conv_w"], p["conv_b"], p["bcol"], p["acol"], p["dskip"])
        att = [_attention(t.reshape(Bn, S // dil, dil * 3 * A_WIDTH), dil)
               for t, (_, dil) in zip((qkv, qkv4, qkv16), ATT_CFGS)]
        xact3 = xact.reshape(Bn, S, C_CONV_CH)
        h = _mix_out(xact3, dtT, yf.reshape(Bn, S, C_WIDTH), zc.reshape(Bn, S, C_WIDTH),
                     zab.reshape(Bn, S, 2048),
                     att[0][0], att[1][0], att[2][0], att[0][1], att[1][1], att[2][1], h,
                     p["bcol"], p["acol"], p["sguw"], p["sgub"], p["e8"],
                     p["nrm"], p["wout"], fnw, final=(li == len(layers) - 1))
    return h


def kernel(x_prompt, x_sample, norm_w, w_in, sgu_w, sgu_b, conv_w, conv_b, dt_bias, a_log, d_skip,
           ssd_norm_w, w_out, final_norm_w):
    depth = w_in.shape[0]
    layers = [_layer_params(l, norm_w, w_in, sgu_w, sgu_b, conv_w, conv_b, dt_bias, a_log, d_skip,
                            ssd_norm_w, w_out) for l in range(depth)]
    fnw = final_norm_w[None, :]
    return (_trunk(x_prompt, layers, fnw), _trunk(x_sample, layers, fnw))
```
